```python
import jax
import jax.numpy as jnp
from jax import lax
import numpy as np

D_MODEL = 4096
BATCH = 1
SEQ = 8192
DEPTH = 1

D_MIX = D_MODEL
D_POOL = D_MIX // 2
POOL_WINDOWS = (2, 4, 8, 16)
N_POOL_GROUPS = len(POOL_WINDOWS)
POOL_GROUP = D_POOL // N_POOL_GROUPS
D_MLSTM = D_MIX - D_POOL
N_HEADS = 8
DV = D_MLSTM // N_HEADS
DQK = DV // 2
D_QK = 2 * N_HEADS * DQK
CONV_W = 4
CHUNK = 64
GATE_CAP = 15.0
D_FF = 4 * D_MODEL
D_PLE = 256
EPS = 1e-6
D_IN = D_POOL + D_QK + D_MLSTM + D_MLSTM + 2 * N_HEADS

kernel_name = 'hybrid_pool_mlstm_layer'


def rmsnorm(x, g):
    xf = x.astype(jnp.float32)
    y = xf * lax.rsqrt(jnp.mean(xf * xf, axis=-1, keepdims=True) + EPS)
    return (y * g.astype(jnp.float32)).astype(x.dtype)


def soft_cap(z):
    return GATE_CAP * jnp.tanh(z / GATE_CAP)


def causal_conv(x, w, b):
    S = x.shape[1]
    xp = jnp.pad(x, ((0, 0), (CONV_W - 1, 0), (0, 0)))
    y = b
    for j in range(CONV_W):
        y = y + w[j] * xp[:, j:j + S]
    return y


def pool_mixer(u, w_pool, scale):
    B, S, _ = u.shape
    uf = u.astype(jnp.float32)
    csp = jnp.concatenate([jnp.zeros((B, 1, D_POOL), jnp.float32), jnp.cumsum(uf, axis=1)], axis=1)
    t = jnp.arange(S)
    outs = []
    for gi, w in enumerate(POOL_WINDOWS):
        sl = slice(gi * POOL_GROUP, (gi + 1) * POOL_GROUP)
        c = csp[..., sl]
        lo = jnp.maximum(t + 1 - w, 0)
        win_sum = c[:, 1:] - c[:, lo]
        cnt = jnp.minimum(t + 1, w).astype(jnp.float32)[None, :, None]
        outs.append(win_sum / cnt - uf[..., sl])
    z = jnp.stack(outs, axis=2).astype(u.dtype)
    y = jnp.einsum('bsgc,gcd->bsgd', z, w_pool).reshape(B, S, D_POOL)
    return y * scale


def mlstm_chunkwise(q, k, v, li, lf):
    B, H, S, _ = q.shape
    NC = S // CHUNK
    q = q.reshape(B, H, NC, CHUNK, DQK)
    k = k.reshape(B, H, NC, CHUNK, DQK)
    v = v.reshape(B, H, NC, CHUNK, DV)
    li = li.reshape(B, H, NC, CHUNK)
    lf = lf.reshape(B, H, NC, CHUNK)
    b = jnp.cumsum(lf, axis=-1)
    g = b[..., -1]
    a = g[..., None] - b + li
    m_loc = jnp.max(a, axis=-1)
    wa = jnp.exp(a - m_loc[..., None])
    C_loc = jnp.einsum('bhcl,bhcld,bhcle->bhcde', wa, k, v)
    n_loc = jnp.einsum('bhcl,bhcld->bhcd', wa, k)

    def step(carry, xs):
        C, n, m = carry
        g_c, m_l, C_l, n_l = xs
        m_new = jnp.maximum(g_c + m, m_l)
        s_old = jnp.exp(g_c + m - m_new)
        s_loc = jnp.exp(m_l - m_new)
        C_new = s_old[..., None, None] * C + s_loc[..., None, None] * C_l
        n_new = s_old[..., None] * n + s_loc[..., None] * n_l
        return (C_new, n_new, m_new), (C, n, m)

    init = (jnp.zeros((B, H, DQK, DV), jnp.float32), jnp.zeros((B, H, DQK), jnp.float32),
            jnp.zeros((B, H), jnp.float32))
    xs = (jnp.moveaxis(g, 2, 0), jnp.moveaxis(m_loc, 2, 0), jnp.moveaxis(C_loc, 2, 0), jnp.moveaxis(n_loc, 2, 0))
    _, (C_prev, n_prev, m_prev) = lax.scan(step, init, xs)
    C_prev = jnp.moveaxis(C_prev, 0, 2)
    n_prev = jnp.moveaxis(n_prev, 0, 2)
    m_prev = jnp.moveaxis(m_prev, 0, 2)

    causal = jnp.tril(jnp.ones((CHUNK, CHUNK), dtype=bool))
    D = b[..., :, None] - b[..., None, :] + li[..., None, :]
    D = jnp.where(causal, D, -jnp.inf)
    e = b + m_prev[..., None]
    m_t = jnp.maximum(jnp.max(D, axis=-1), e)
    W = jnp.exp(D - m_t[..., None]) * jnp.einsum('bhctd,bhcsd->bhcts', q, k)
    w_inter = jnp.exp(e - m_t)
    num = jnp.einsum('bhcts,bhcse->bhcte', W, v) + w_inter[..., None] * jnp.einsum('bhctd,bhcde->bhcte', q, C_prev)
    den = jnp.sum(W, axis=-1) + w_inter * jnp.einsum('bhctd,bhcd->bhct', q, n_prev)
    h = num / jnp.maximum(jnp.abs(den), jnp.exp(-m_t))[..., None]
    return h.reshape(B, H, S, DV)


def mlstm_mixer(qk_pre, v, o_pre, gate_pre, conv_w, conv_b, b_i, b_f, g_head):
    B, S, _ = v.shape
    f32 = jnp.float32
    qk = jax.nn.silu(causal_conv(qk_pre, conv_w, conv_b)).astype(f32)
    qk = qk.reshape(B, S, 2, N_HEADS, DQK).transpose(2, 0, 3, 1, 4)
    q = qk[0]
    k = qk[1] * (DQK ** -0.5)
    vh = v.astype(f32).reshape(B, S, N_HEADS, DV).transpose(0, 2, 1, 3)
    gates = gate_pre.astype(f32)
    i_pre = soft_cap(gates[..., :N_HEADS] + b_i.astype(f32))
    f_pre = soft_cap(gates[..., N_HEADS:] + b_f.astype(f32))
    li = i_pre.transpose(0, 2, 1)
    lf = jax.nn.log_sigmoid(f_pre).transpose(0, 2, 1)
    h = mlstm_chunkwise(q, k, vh, li, lf)
    h = h * lax.rsqrt(jnp.mean(h * h, axis=-1, keepdims=True) + EPS)
    h = h * g_head.astype(f32).reshape(N_HEADS, 1, DV)
    h = h.transpose(0, 2, 1, 3).reshape(B, S, D_MLSTM)
    return (jax.nn.sigmoid(o_pre.astype(f32)) * h).astype(v.dtype)


def setup_inputs(seed: int = 0) -> dict:
    key = jax.random.key(seed)
    ks = jax.random.split(key, 20)
    f32 = jnp.float32
    nrm = lambda k, shape, s: jax.random.normal(k, shape, f32) * s
    return {
        'x': nrm(ks[0], (BATCH, SEQ, D_MODEL), 1.0),
        'p': nrm(ks[1], (DEPTH, BATCH, SEQ, D_PLE), 1.0),
        'g_mix': 1.0 + nrm(ks[2], (DEPTH, D_MODEL), 0.02),
        'w_in': nrm(ks[3], (DEPTH, D_MODEL, D_IN), D_MODEL ** -0.5),
        'conv_w': nrm(ks[4], (DEPTH, CONV_W, D_QK), CONV_W ** -0.5),
        'conv_b': nrm(ks[5], (DEPTH, D_QK), 0.02),
        'b_igate': nrm(ks[6], (DEPTH, N_HEADS), 0.1),
        'b_fgate': jnp.broadcast_to(jnp.linspace(3.0, 6.0, N_HEADS, dtype=f32), (DEPTH, N_HEADS)) + nrm(ks[7], (DEPTH, N_HEADS), 0.01),
        'g_head': 1.0 + nrm(ks[8], (DEPTH, D_MLSTM), 0.02),
        'w_pool': nrm(ks[9], (DEPTH, N_POOL_GROUPS, POOL_GROUP, POOL_GROUP), POOL_GROUP ** -0.5),
        'pool_scale': 1.0 + nrm(ks[10], (DEPTH, D_POOL), 0.1),
        'w_out': nrm(ks[11], (DEPTH, D_MIX, D_MODEL), D_MIX ** -0.5),
        'g_mlp': 1.0 + nrm(ks[12], (DEPTH, D_MODEL), 0.02),
        'w_up': nrm(ks[13], (DEPTH, D_MODEL, D_FF), D_MODEL ** -0.5),
        'w_down': nrm(ks[14], (DEPTH, D_FF, D_MODEL), D_FF ** -0.5),
        'g_ple': 1.0 + nrm(ks[15], (DEPTH, D_MODEL), 0.02),
        'w_ple_gate': nrm(ks[16], (DEPTH, D_MODEL, D_MODEL), D_MODEL ** -0.5),
        'b_ple_gate': nrm(ks[17], (DEPTH, D_MODEL), 0.02),
        'w_ple': nrm(ks[18], (DEPTH, D_PLE, D_MODEL), D_PLE ** -0.5),
        'g_final': 1.0 + nrm(ks[19], (D_MODEL,), 0.02),
    }


def reference(x, p, g_mix, w_in, conv_w, conv_b, b_igate, b_fgate, g_head, w_pool, pool_scale,
              w_out, g_mlp, w_up, w_down, g_ple, w_ple_gate, b_ple_gate, w_ple, g_final):
    o0 = D_POOL
    o1 = o0 + D_QK
    o2 = o1 + D_MLSTM
    o3 = o2 + D_MLSTM
    h = x
    for l in range(DEPTH):
        hn = rmsnorm(h, g_mix[l])
        proj = hn @ w_in[l]
        y_pool = pool_mixer(proj[..., :o0], w_pool[l], pool_scale[l])
        y_mlstm = mlstm_mixer(proj[..., o0:o1], proj[..., o1:o2], proj[..., o2:o3], proj[..., o3:],
                              conv_w[l], conv_b[l], b_igate[l], b_fgate[l], g_head[l])
        h = h + jnp.concatenate([y_pool, y_mlstm], axis=-1) @ w_out[l]
        hn = rmsnorm(h, g_mlp[l])
        h = h + jnp.square(jax.nn.relu(hn @ w_up[l])) @ w_down[l]
        hn = rmsnorm(h, g_ple[l])
        gate = jax.nn.sigmoid(hn @ w_ple_gate[l] + b_ple_gate[l])
        h = h + gate * (p[l] @ w_ple[l])
    return rmsnorm(h, g_final)
```

```python
import functools

import jax
import jax.numpy as jnp
from jax import lax
from jax.experimental import pallas as pl
from jax.experimental.pallas import tpu as pltpu

F32 = jnp.float32
BF16 = jnp.bfloat16

D_MODEL = 4096
SEQ = 8192
D_POOL = 2048
POOL_WINDOWS = (2, 4, 8, 16)
POOL_GROUP = 512
POOL_HALO = 16
D_MLSTM = 2048
N_HEADS = 8
DV = 256
DQK = 128
D_QK = 2 * N_HEADS * DQK
CONV_W = 4
CONV_HALO = 8
GATE_CAP = 15.0
D_FF = 4 * D_MODEL
D_PLE = 256
EPS = 1e-6
D_MAIN = D_POOL + D_QK + 2 * D_MLSTM
LANES = 128
CHUNK = 128
VMEM_LIMIT = 56 * 1024 * 1024


def _params(sem, vmem=VMEM_LIMIT):
    return pltpu.CompilerParams(dimension_semantics=sem, vmem_limit_bytes=vmem)


def _rmsnorm_kernel(x_ref, g_ref, o_ref):
    x = x_ref[...]
    ms = jnp.mean(x * x, axis=-1, keepdims=True)
    o_ref[...] = (x * lax.rsqrt(ms + EPS) * g_ref[...]).astype(o_ref.dtype)


def _rmsnorm(x, g, out_dtype, name, tm=256):
    s, d = x.shape
    return pl.pallas_call(
        _rmsnorm_kernel,
        grid=(s // tm,),
        in_specs=[pl.BlockSpec((tm, d), lambda i: (i, 0)), pl.BlockSpec((1, d), lambda i: (0, 0))],
        out_specs=pl.BlockSpec((tm, d), lambda i: (i, 0)),
        out_shape=jax.ShapeDtypeStruct((s, d), out_dtype),
        compiler_params=_params(("parallel",)),
        name=name,
    )(x, g.reshape(1, d).astype(F32))


def _mm_kernel(a_ref, w_ref, o_ref):
    o_ref[...] = jnp.dot(a_ref[...], w_ref[...], preferred_element_type=F32).astype(o_ref.dtype)


def _mm_relu2_kernel(a_ref, w_ref, o_ref):
    y = jnp.maximum(jnp.dot(a_ref[...], w_ref[...], preferred_element_type=F32), 0.0)
    o_ref[...] = (y * y).astype(o_ref.dtype)


def _matmul(kern, a, w, n_out, out_dtype, name, tm=1024, tn=1024):
    m, k = a.shape
    return pl.pallas_call(
        kern,
        grid=(m // tm, n_out // tn),
        in_specs=[pl.BlockSpec((tm, k), lambda i, j: (i, 0)), pl.BlockSpec((k, tn), lambda i, j: (0, j))],
        out_specs=pl.BlockSpec((tm, tn), lambda i, j: (i, j)),
        out_shape=jax.ShapeDtypeStruct((m, n_out), out_dtype),
        compiler_params=_params(("parallel", "arbitrary")),
        name=name,
    )(a, w)


def _mix_out_kernel(yp_ref, ym_ref, wp_ref, wm_ref, r_ref, o_ref):
    acc = jnp.dot(yp_ref[...], wp_ref[...], preferred_element_type=F32)
    acc += jnp.dot(ym_ref[...], wm_ref[...], preferred_element_type=F32)
    o_ref[...] = r_ref[...] + acc


def _mix_out(y_pool, y_mlstm, w_out, resid, tm=1024, tn=1024):
    m = resid.shape[0]
    return pl.pallas_call(
        _mix_out_kernel,
        grid=(m // tm, D_MODEL // tn),
        in_specs=[
            pl.BlockSpec((tm, D_POOL), lambda i, j: (i, 0)),
            pl.BlockSpec((tm, D_MLSTM), lambda i, j: (i, 0)),
            pl.BlockSpec((D_POOL, tn), lambda i, j: (0, j)),
            pl.BlockSpec((D_MLSTM, tn), lambda i, j: (1, j)),
            pl.BlockSpec((tm, tn), lambda i, j: (i, j)),
        ],
        out_specs=pl.BlockSpec((tm, tn), lambda i, j: (i, j)),
        out_shape=jax.ShapeDtypeStruct((m, D_MODEL), F32),
        compiler_params=_params(("parallel", "arbitrary")),
        name="mix_out",
    )(y_pool, y_mlstm, w_out, w_out, resid)


def _down_kernel(a_ref, w_ref, r_ref, o_ref):
    part = jnp.dot(a_ref[...], w_ref[...], preferred_element_type=F32)

    @pl.when(pl.program_id(2) == 0)
    def _():
        o_ref[...] = r_ref[...] + part

    @pl.when(pl.program_id(2) != 0)
    def _():
        o_ref[...] += part


def _down(a, w, resid, tm=1024, tn=1024, tk=2048):
    m, k = a.shape
    n = w.shape[1]
    return pl.pallas_call(
        _down_kernel,
        grid=(m // tm, n // tn, k // tk),
        in_specs=[
            pl.BlockSpec((tm, tk), lambda i, j, kk: (i, kk)),
            pl.BlockSpec((tk, tn), lambda i, j, kk: (kk, j)),
            pl.BlockSpec((tm, tn), lambda i, j, kk: (i, j)),
        ],
        out_specs=pl.BlockSpec((tm, tn), lambda i, j, kk: (i, j)),
        out_shape=jax.ShapeDtypeStruct((m, n), F32),
        compiler_params=_params(("parallel", "arbitrary", "arbitrary")),
        name="mlp_down",
    )(a, w, resid)


def _ple_kernel(a_ref, w_ref, b_ref, p_ref, wp_ref, r_ref, o_ref):
    z = jnp.dot(a_ref[...], w_ref[...], preferred_element_type=F32) + b_ref[...]
    pe = jnp.dot(p_ref[...], wp_ref[...], preferred_element_type=F32)
    o_ref[...] = r_ref[...] + jax.nn.sigmoid(z) * pe


def _ple(hn, w_gate, b_gate, p, w_ple, resid, tm=512, tn=1024):
    m, k = hn.shape
    return pl.pallas_call(
        _ple_kernel,
        grid=(m // tm, D_MODEL // tn),
        in_specs=[
            pl.BlockSpec((tm, k), lambda i, j: (i, 0)),
            pl.BlockSpec((k, tn), lambda i, j: (0, j)),
            pl.BlockSpec((1, tn), lambda i, j: (0, j)),
            pl.BlockSpec((tm, D_PLE), lambda i, j: (i, 0)),
            pl.BlockSpec((D_PLE, tn), lambda i, j: (0, j)),
            pl.BlockSpec((tm, tn), lambda i, j: (i, j)),
        ],
        out_specs=pl.BlockSpec((tm, tn), lambda i, j: (i, j)),
        out_shape=jax.ShapeDtypeStruct((m, D_MODEL), F32),
        compiler_params=_params(("parallel", "arbitrary")),
        name="ple",
    )(hn, w_gate, b_gate, p, w_ple, resid)


def _chunk_scan(x, pos, combine, fill):
    sh = 1
    while sh < CHUNK:
        prev = jnp.where(pos >= sh, pltpu.roll(x, sh, 0), fill)
        x = combine(x, prev)
        sh *= 2
    return x


def _gates_kernel(a_ref, w_ref, bias_ref, li_ref, b_ref, mx_ref, lirow_ref, brow_ref):
    tm = a_ref.shape[0]
    pre = jnp.dot(a_ref[...], w_ref[...], preferred_element_type=F32) + bias_ref[...]
    pre = GATE_CAP * jnp.tanh(pre / GATE_CAP)
    logf = jnp.minimum(pre, 0.0) - jnp.log1p(jnp.exp(-jnp.abs(pre)))
    pos = lax.broadcasted_iota(jnp.int32, (tm, LANES), 0) % CHUNK
    b = _chunk_scan(logf, pos, jnp.add, 0.0)
    b = pltpu.roll(b, LANES - N_HEADS, 1)
    li = pre
    mx = b + _chunk_scan(li - b, pos, jnp.maximum, -jnp.inf)
    li_ref[...] = li
    b_ref[...] = b
    mx_ref[...] = mx
    for c in range(tm // CHUNK):
        rows = slice(c * CHUNK, (c + 1) * CHUNK)
        lirow_ref[c] = li[rows].T[:N_HEADS]
        brow_ref[c] = b[rows].T[:N_HEADS]


def _gates(hn, w_g, bias, tm=1024):
    m, k = hn.shape
    col = jax.ShapeDtypeStruct((m, LANES), F32)
    row = jax.ShapeDtypeStruct((m // CHUNK, N_HEADS, CHUNK), F32)
    col_spec = pl.BlockSpec((tm, LANES), lambda i: (i, 0))
    row_spec = pl.BlockSpec((tm // CHUNK, N_HEADS, CHUNK), lambda i: (i, 0, 0))
    return pl.pallas_call(
        _gates_kernel,
        grid=(m // tm,),
        in_specs=[
            pl.BlockSpec((tm, k), lambda i: (i, 0)),
            pl.BlockSpec((k, LANES), lambda i: (0, 0)),
            pl.BlockSpec((1, LANES), lambda i: (0, 0)),
        ],
        out_specs=[col_spec, col_spec, col_spec, row_spec, row_spec],
        out_shape=[col, col, col, row, row],
        compiler_params=_params(("parallel",)),
        name="gates",
    )(hn, w_g, bias)


def _pool_kernel(u_ref, w_ref, scale_ref, o_ref, ext_ref):
    ts = u_ref.shape[0]
    i = pl.program_id(0)

    @pl.when(i == 0)
    def _():
        ext_ref[0:POOL_HALO, :] = jnp.zeros((POOL_HALO, D_POOL), F32)

    @pl.when(i != 0)
    def _():
        ext_ref[0:POOL_HALO, :] = ext_ref[ts:ts + POOL_HALO, :]

    ext_ref[POOL_HALO:POOL_HALO + ts, :] = u_ref[...]
    t = i * ts + lax.broadcasted_iota(jnp.int32, (ts, 1), 0)
    for g, win in enumerate(POOL_WINDOWS):
        cols = slice(g * POOL_GROUP, (g + 1) * POOL_GROUP)
        u = ext_ref[POOL_HALO:POOL_HALO + ts, cols]
        win_sum = u
        for j in range(1, win):
            win_sum = win_sum + ext_ref[POOL_HALO - j:POOL_HALO - j + ts, cols]
        cnt = jnp.minimum(t + 1, win).astype(F32)
        z = (win_sum / cnt - u).astype(BF16)
        y = jnp.dot(z, w_ref[g], preferred_element_type=F32)
        o_ref[:, cols] = (y * scale_ref[:, cols]).astype(o_ref.dtype)


def _pool(proj, w_pool, scale, ts=512):
    s = proj.shape[0]
    return pl.pallas_call(
        _pool_kernel,
        grid=(s // ts,),
        in_specs=[
            pl.BlockSpec((ts, D_POOL), lambda i: (i, 0)),
            pl.BlockSpec((len(POOL_WINDOWS), POOL_GROUP, POOL_GROUP), lambda i: (0, 0, 0)),
            pl.BlockSpec((1, D_POOL), lambda i: (0, 0)),
        ],
        out_specs=pl.BlockSpec((ts, D_POOL), lambda i: (i, 0)),
        out_shape=jax.ShapeDtypeStruct((s, D_POOL), BF16),
        scratch_shapes=[pltpu.VMEM((ts + POOL_HALO, D_POOL), F32)],
        compiler_params=_params(("arbitrary",)),
        name="pool_mixer",
    )(proj, w_pool, scale)


def _mlstm_kernel(qk_ref, v_ref, o_ref, li_ref, b_ref, mx_ref, lirow_ref, brow_ref,
                  cw_ref, cb_ref, gh_ref, y_ref, ext_ref, c_ref, n_ref, m_ref):
    L = CHUNK
    i = pl.program_id(0)

    @pl.when(i == 0)
    def _():
        ext_ref[0:CONV_HALO, :] = jnp.zeros((CONV_HALO, D_QK), F32)
        c_ref[...] = jnp.zeros_like(c_ref)
        n_ref[...] = jnp.zeros_like(n_ref)
        m_ref[...] = jnp.zeros_like(m_ref)

    @pl.when(i != 0)
    def _():
        ext_ref[0:CONV_HALO, :] = ext_ref[L:L + CONV_HALO, :]

    ext_ref[CONV_HALO:CONV_HALO + L, :] = qk_ref[...]

    t_idx = lax.broadcasted_iota(jnp.int32, (L, L), 0)
    s_idx = lax.broadcasted_iota(jnp.int32, (L, L), 1)
    causal = s_idx <= t_idx

    def conv_silu(cols):
        y = cb_ref[:, cols]
        for j in range(CONV_W):
            off = CONV_HALO - (CONV_W - 1) + j
            y = y + cw_ref[j:j + 1, cols] * ext_ref[off:off + L, cols]
        return y * jax.nn.sigmoid(y)

    for h in range(N_HEADS):
        q = conv_silu(slice(h * DQK, (h + 1) * DQK))
        k = conv_silu(slice(N_HEADS * DQK + h * DQK, N_HEADS * DQK + (h + 1) * DQK)) * (DQK ** -0.5)
        vcols = slice(h * DV, (h + 1) * DV)
        v = v_ref[:, vcols].astype(BF16)
        li_col = li_ref[:, h:h + 1]
        b_col = b_ref[:, h:h + 1]
        mx_col = mx_ref[:, h:h + 1]
        li_row = lirow_ref[0, h:h + 1, :]
        b_row = brow_ref[0, h:h + 1, :]
        m_prev = m_ref[h][:, 0:1]
        c_prev = c_ref[h]
        n_prev = n_ref[h]

        qb = q.astype(BF16)
        kb = k.astype(BF16)
        d_mat = jnp.where(causal, b_col - b_row + li_row, -jnp.inf)
        e_col = b_col + m_prev
        m_t = jnp.maximum(mx_col, e_col)
        s_qk = lax.dot_general(qb, kb, (((1,), (1,)), ((), ())), preferred_element_type=F32)
        w_mat = jnp.exp(d_mat - m_t) * s_qk
        w_inter = jnp.exp(e_col - m_t)
        num = jnp.dot(w_mat.astype(BF16), v, preferred_element_type=F32)
        num = num + w_inter * jnp.dot(qb, c_prev.astype(BF16), preferred_element_type=F32)
        den = jnp.sum(w_mat, axis=-1, keepdims=True) + w_inter * jnp.sum(q * n_prev, axis=-1, keepdims=True)
        hh = num / jnp.maximum(jnp.abs(den), jnp.exp(-m_t))
        hh = hh * lax.rsqrt(jnp.mean(hh * hh, axis=-1, keepdims=True) + EPS)
        hh = hh * gh_ref[:, vcols] * jax.nn.sigmoid(o_ref[:, vcols])
        y_ref[:, vcols] = hh.astype(y_ref.dtype)

        g_tot = b_col[L - 1:L, :]
        a_col = g_tot + li_col - b_col
        m_loc = jnp.max(a_col, axis=0, keepdims=True)
        kw = k * jnp.exp(a_col - m_loc)
        c_loc = lax.dot_general(kw.astype(BF16), v, (((0,), (0,)), ((), ())), preferred_element_type=F32)
        n_loc = jnp.sum(kw, axis=0, keepdims=True)
        m_new = jnp.maximum(g_tot + m_prev, m_loc)
        s_old = jnp.exp(g_tot + m_prev - m_new)
        s_loc = jnp.exp(m_loc - m_new)
        c_ref[h] = s_old * c_prev + s_loc * c_loc
        n_ref[h] = s_old * n_prev + s_loc * n_loc
        m_ref[h] = jnp.broadcast_to(m_new, (1, LANES))


def _mlstm(proj, li, b, mx, lirow, brow, conv_w, conv_b, g_head):
    s = proj.shape[0]
    L = CHUNK
    col_spec = pl.BlockSpec((L, LANES), lambda i: (i, 0))
    row_spec = pl.BlockSpec((1, N_HEADS, L), lambda i: (i, 0, 0))
    return pl.pallas_call(
        _mlstm_kernel,
        grid=(s // L,),
        in_specs=[
            pl.BlockSpec((L, D_QK), lambda i: (i, D_POOL // D_QK)),
            pl.BlockSpec((L, D_MLSTM), lambda i: (i, (D_POOL + D_QK) // D_MLSTM)),
            pl.BlockSpec((L, D_MLSTM), lambda i: (i, (D_POOL + D_QK + D_MLSTM) // D_MLSTM)),
            col_spec, col_spec, col_spec, row_spec, row_spec,
            pl.BlockSpec((CONV_W, D_QK), lambda i: (0, 0)),
            pl.BlockSpec((1, D_QK), lambda i: (0, 0)),
            pl.BlockSpec((1, D_MLSTM), lambda i: (0, 0)),
        ],
        out_specs=pl.BlockSpec((L, D_MLSTM), lambda i: (i, 0)),
        out_shape=jax.ShapeDtypeStruct((s, D_MLSTM), BF16),
        scratch_shapes=[
            pltpu.VMEM((L + CONV_HALO, D_QK), F32),
            pltpu.VMEM((N_HEADS, DQK, DV), F32),
            pltpu.VMEM((N_HEADS, 1, DQK), F32),
            pltpu.VMEM((N_HEADS, 1, LANES), F32),
        ],
        compiler_params=_params(("arbitrary",)),
        name="mlstm_mixer",
    )(proj, proj, proj, li, b, mx, lirow, brow, conv_w, conv_b, g_head)


def kernel(x, p, g_mix, w_in, conv_w, conv_b, b_igate, b_fgate, g_head, w_pool, pool_scale,
           w_out, g_mlp, w_up, w_down, g_ple, w_ple_gate, b_ple_gate, w_ple, g_final):
    depth = w_in.shape[0]
    h = x.reshape(SEQ, D_MODEL)
    for l in range(depth):
        w_main = w_in[l, :, :D_MAIN].astype(BF16)
        w_g = jnp.pad(w_in[l, :, D_MAIN:], ((0, 0), (0, LANES - 2 * N_HEADS))).astype(BF16)
        gate_bias = jnp.pad(jnp.concatenate([b_igate[l], b_fgate[l]]), (0, LANES - 2 * N_HEADS)).reshape(1, LANES)

        hn = _rmsnorm(h, g_mix[l], BF16, "norm_mix")
        proj = _matmul(_mm_kernel, hn, w_main, D_MAIN, F32, "proj_in")
        li, b, mx, lirow, brow = _gates(hn, w_g, gate_bias.astype(F32))
        y_pool = _pool(proj, w_pool[l].astype(BF16), pool_scale[l].reshape(1, D_POOL))
        y_mlstm = _mlstm(proj, li, b, mx, lirow, brow, conv_w[l], conv_b[l].reshape(1, D_QK),
                         g_head[l].reshape(1, D_MLSTM))
        h = _mix_out(y_pool, y_mlstm, w_out[l].astype(BF16), h)

        hn = _rmsnorm(h, g_mlp[l], BF16, "norm_mlp")
        act = _matmul(_mm_relu2_kernel, hn, w_up[l].astype(BF16), D_FF, BF16, "mlp_up")
        h = _down(act, w_down[l].astype(BF16), h)

        hn = _rmsnorm(h, g_ple[l], BF16, "norm_ple")
        h = _ple(hn, w_ple_gate[l].astype(BF16), b_ple_gate[l].reshape(1, D_MODEL),
                 p[l].reshape(SEQ, D_PLE).astype(BF16), w_ple[l].astype(BF16), h)
    out = _rmsnorm(h, g_final, F32, "norm_final")
    return out.reshape(x.shape)
```

```python
import jax
import jax.numpy as jnp
from jax import lax
from jax.experimental import pallas as pl
from jax.experimental.pallas import tpu as pltpu

F32 = jnp.float32
BF16 = jnp.bfloat16

D_MODEL = 4096
SEQ = 8192
D_POOL = 2048
POOL_WINDOWS = (2, 4, 8, 16)
POOL_GROUP = 512
POOL_HALO = 16
D_MLSTM = 2048
N_HEADS = 8
DV = 256
DQK = 128
D_QK = 2 * N_HEADS * DQK
CONV_W = 4
CONV_HALO = 8
GATE_CAP = 15.0
D_FF = 4 * D_MODEL
D_PLE = 256
EPS = 1e-6
D_MAIN = D_POOL + D_QK + 2 * D_MLSTM
LANES = 128
CHUNK = 128
VMEM_LIMIT = 58 * 1024 * 1024


def _params(sem, vmem=VMEM_LIMIT):
    return pltpu.CompilerParams(dimension_semantics=sem, vmem_limit_bytes=vmem)


def _rmsnorm_kernel(x_ref, g_ref, o_ref):
    x = x_ref[...]
    ms = jnp.mean(x * x, axis=-1, keepdims=True)
    o_ref[...] = (x * lax.rsqrt(ms + EPS) * g_ref[...]).astype(o_ref.dtype)


def _rmsnorm(x, g, out_dtype, name, tm=256):
    s, d = x.shape
    return pl.pallas_call(
        _rmsnorm_kernel,
        grid=(s // tm,),
        in_specs=[pl.BlockSpec((tm, d), lambda i: (i, 0)), pl.BlockSpec((1, d), lambda i: (0, 0))],
        out_specs=pl.BlockSpec((tm, d), lambda i: (i, 0)),
        out_shape=jax.ShapeDtypeStruct((s, d), out_dtype),
        compiler_params=_params(("parallel",)),
        name=name,
    )(x, g.reshape(1, d).astype(F32))


def _ws_maps(nb, n_i):
    row = lambda b, i: jnp.where(b > 0, i, 0)
    blk = lambda b: jnp.maximum(b - 1, 0)
    wrow = lambda b, i: jnp.where(b < nb, i, n_i - 1)
    wblk = lambda b: jnp.minimum(b, nb - 1)
    return row, blk, wrow, wblk


def _ws_step(wchunk_ref, g_ref, wbuf_ref, compute):
    b = pl.program_id(0)
    i = pl.program_id(1)
    ck = wchunk_ref.shape[0]

    def stage():
        w = wchunk_ref[...]
        if g_ref is not None:
            w = w * g_ref[...]
        wbuf_ref[b % 2, pl.ds(pl.multiple_of(i * ck, ck), ck), :] = w.astype(BF16)

    @pl.when(b == 0)
    def _():
        stage()

    @pl.when(b > 0)
    def _():
        stage()
        compute((b + 1) % 2)


def _row_sumsq(h):
    return jnp.broadcast_to(jnp.sum(h * h, axis=1, keepdims=True), (h.shape[0], LANES))


def _lane_tile(r, n):
    return jnp.concatenate([r] * (n // LANES), axis=1)


def _proj_kernel(a_ref, wchunk_ref, o_ref, wbuf_ref):
    def compute(slot):
        o_ref[...] = jnp.dot(a_ref[...], wbuf_ref[slot], preferred_element_type=F32)
    _ws_step(wchunk_ref, None, wbuf_ref, compute)


def _mix_out_kernel(yp_ref, ym_ref, wchunk_ref, res_ref, o_ref, ob_ref, ssq_ref, wbuf_ref):
    def compute(slot):
        acc = jnp.dot(yp_ref[...], wbuf_ref[slot, 0:D_POOL, :], preferred_element_type=F32)
        acc += jnp.dot(ym_ref[...], wbuf_ref[slot, D_POOL:D_POOL + D_MLSTM, :], preferred_element_type=F32)
        h = res_ref[...] + acc
        o_ref[...] = h
        ob_ref[...] = h.astype(BF16)
        ssq_ref[...] = _row_sumsq(h)
    _ws_step(wchunk_ref, None, wbuf_ref, compute)


def _up_kernel(a_ref, wchunk_ref, g_ref, r_ref, wd_ref, o_ref, wdb_ref, wbuf_ref):
    def compute(slot):
        y = jnp.dot(a_ref[...], wbuf_ref[slot], preferred_element_type=F32)
        y = jnp.maximum(y * _lane_tile(r_ref[...], y.shape[1]), 0.0)
        o_ref[...] = (y * y).astype(BF16)
        wdb_ref[...] = wd_ref[...].astype(BF16)
    _ws_step(wchunk_ref, g_ref, wbuf_ref, compute)


def _ple_kernel(a_ref, wchunk_ref, g_ref, r_ref, bias_ref, p_ref, wp_ref, res_ref, o_ref, ssq_ref, wbuf_ref):
    def compute(slot):
        z = jnp.dot(a_ref[...], wbuf_ref[slot], preferred_element_type=F32)
        z = z * _lane_tile(r_ref[...], z.shape[1]) + bias_ref[...]
        pe = jnp.dot(p_ref[...], wp_ref[...], preferred_element_type=F32)
        h = res_ref[...] + jax.nn.sigmoid(z) * pe
        o_ref[...] = h
        ssq_ref[...] = _row_sumsq(h)
    _ws_step(wchunk_ref, g_ref, wbuf_ref, compute)


def _ws_call(kern, name, l, a_list, w, k_dim, n_out, tm, tn, extra_in, extra_specs, out_shapes, out_specs):
    m = a_list[0].shape[0]
    nb, n_i = n_out // tn, m // tm
    ck = k_dim // n_i
    row, blk, wrow, wblk = _ws_maps(nb, n_i)
    a_specs = [pl.BlockSpec((tm, a.shape[1]), lambda b, i: (row(b, i), 0)) for a in a_list]
    w_spec = pl.BlockSpec((None, ck, tn), lambda b, i: (l, wrow(b, i), wblk(b)))
    return pl.pallas_call(
        kern,
        grid=(nb + 1, n_i),
        in_specs=a_specs + [w_spec] + extra_specs(row, blk, wrow, wblk),
        out_specs=out_specs(row, blk),
        out_shape=out_shapes,
        scratch_shapes=[pltpu.VMEM((2, k_dim, tn), BF16)],
        compiler_params=_params(("arbitrary", "arbitrary")),
        name=name,
    )(*a_list, w, *extra_in)


def _proj_in(l, hn, w_in, tm=1024, tn=1024):
    m = hn.shape[0]
    return _ws_call(
        _proj_kernel, "proj_in", l, [hn], w_in, D_MODEL, D_MAIN, tm, tn, [],
        lambda row, blk, wrow, wblk: [],
        jax.ShapeDtypeStruct((m, D_MAIN), F32),
        lambda row, blk: pl.BlockSpec((tm, tn), lambda b, i: (row(b, i), blk(b))))


def _mix_out(l, y_pool, y_mlstm, w_out, resid, tm=512, tn=1024):
    m = resid.shape[0]
    nb = D_MODEL // tn
    tile = lambda row, blk: pl.BlockSpec((tm, tn), lambda b, i: (row(b, i), blk(b)))
    return _ws_call(
        _mix_out_kernel, "mix_out", l, [y_pool, y_mlstm], w_out, D_MODEL, D_MODEL, tm, tn, [resid],
        lambda row, blk, wrow, wblk: [tile(row, blk)],
        [jax.ShapeDtypeStruct((m, D_MODEL), F32), jax.ShapeDtypeStruct((m, D_MODEL), BF16),
         jax.ShapeDtypeStruct((m, nb * LANES), F32)],
        lambda row, blk: [tile(row, blk), tile(row, blk),
                          pl.BlockSpec((tm, LANES), lambda b, i: (row(b, i), blk(b)))])


def _mlp_up(l, hb, w_up, g, r, w_down, tm=1024, tn=1024):
    m = hb.shape[0]
    nb, n_i = D_FF // tn, m // tm
    ck = D_MODEL // n_i
    wd_rows = D_FF // (nb * n_i)
    side = lambda row, blk: (lambda b, i: (blk(b) * n_i + row(b, i), 0))
    return _ws_call(
        _up_kernel, "mlp_up", l, [hb], w_up, D_MODEL, D_FF, tm, tn, [g, r, w_down],
        lambda row, blk, wrow, wblk: [
            pl.BlockSpec((ck, 1), lambda b, i: (wrow(b, i), 0)),
            pl.BlockSpec((tm, LANES), lambda b, i: (row(b, i), 0)),
            pl.BlockSpec((None, wd_rows, D_MODEL), lambda b, i: (l,) + side(row, blk)(b, i)),
        ],
        [jax.ShapeDtypeStruct((m, D_FF), BF16), jax.ShapeDtypeStruct((D_FF, D_MODEL), BF16)],
        lambda row, blk: [pl.BlockSpec((tm, tn), lambda b, i: (row(b, i), blk(b))),
                          pl.BlockSpec((wd_rows, D_MODEL), side(row, blk))])


def _ple(l, hb, w_gate, g, r, b_gate, p, w_ple, resid, tm=512, tn=1024):
    m = hb.shape[0]
    nb, n_i = D_MODEL // tn, m // tm
    ck = D_MODEL // n_i
    tile = lambda row, blk: pl.BlockSpec((tm, tn), lambda b, i: (row(b, i), blk(b)))
    return _ws_call(
        _ple_kernel, "ple", l, [hb], w_gate, D_MODEL, D_MODEL, tm, tn, [g, r, b_gate, p, w_ple, resid],
        lambda row, blk, wrow, wblk: [
            pl.BlockSpec((ck, 1), lambda b, i: (wrow(b, i), 0)),
            pl.BlockSpec((tm, LANES), lambda b, i: (row(b, i), 0)),
            pl.BlockSpec((1, tn), lambda b, i: (0, blk(b))),
            pl.BlockSpec((tm, D_PLE), lambda b, i: (row(b, i), 0)),
            pl.BlockSpec((D_PLE, tn), lambda b, i: (0, blk(b))),
            tile(row, blk),
        ],
        [jax.ShapeDtypeStruct((m, D_MODEL), F32), jax.ShapeDtypeStruct((m, nb * LANES), F32)],
        lambda row, blk: [tile(row, blk), pl.BlockSpec((tm, LANES), lambda b, i: (row(b, i), blk(b)))])


def _down_kernel(a_ref, w_ref, r_ref, o_ref, ob_ref, ssq_ref):
    k = pl.program_id(2)
    part = jnp.dot(a_ref[...], w_ref[...], preferred_element_type=F32)

    @pl.when(k == 0)
    def _():
        o_ref[...] = r_ref[...] + part

    @pl.when(jnp.logical_and(k != 0, k != pl.num_programs(2) - 1))
    def _():
        o_ref[...] += part

    @pl.when(k == pl.num_programs(2) - 1)
    def _():
        h = o_ref[...] + part
        o_ref[...] = h
        ob_ref[...] = h.astype(BF16)
        ssq_ref[...] = _row_sumsq(h)


def _down(a, w, resid, tm=1024, tn=1024, tk=2048):
    m, k = a.shape
    n = w.shape[1]
    tile = pl.BlockSpec((tm, tn), lambda i, j, kk: (i, j))
    return pl.pallas_call(
        _down_kernel,
        grid=(m // tm, n // tn, k // tk),
        in_specs=[
            pl.BlockSpec((tm, tk), lambda i, j, kk: (i, kk)),
            pl.BlockSpec((tk, tn), lambda i, j, kk: (kk, j)),
            tile,
        ],
        out_specs=[tile, tile, pl.BlockSpec((tm, LANES), lambda i, j, kk: (i, j))],
        out_shape=[jax.ShapeDtypeStruct((m, n), F32), jax.ShapeDtypeStruct((m, n), BF16),
                   jax.ShapeDtypeStruct((m, (n // tn) * LANES), F32)],
        compiler_params=_params(("parallel", "arbitrary", "arbitrary")),
        name="mlp_down",
    )(a, w, resid)


def _row_scale_kernel(ssq_ref, r_ref):
    s = ssq_ref[:, 0:LANES]
    for j in range(1, ssq_ref.shape[1] // LANES):
        s = s + ssq_ref[:, j * LANES:(j + 1) * LANES]
    r_ref[...] = lax.rsqrt(s * (1.0 / D_MODEL) + EPS)


def _row_scale(ssq, name, tm=1024):
    m, w = ssq.shape
    return pl.pallas_call(
        _row_scale_kernel,
        grid=(m // tm,),
        in_specs=[pl.BlockSpec((tm, w), lambda i: (i, 0))],
        out_specs=pl.BlockSpec((tm, LANES), lambda i: (i, 0)),
        out_shape=jax.ShapeDtypeStruct((m, LANES), F32),
        compiler_params=_params(("parallel",)),
        name=name,
    )(ssq)


def _scale_kernel(x_ref, r_ref, g_ref, o_ref):
    o_ref[...] = x_ref[...] * r_ref[:, 0:1] * g_ref[...]


def _final_norm(x, r, g, tm=256):
    s, d = x.shape
    return pl.pallas_call(
        _scale_kernel,
        grid=(s // tm,),
        in_specs=[pl.BlockSpec((tm, d), lambda i: (i, 0)), pl.BlockSpec((tm, LANES), lambda i: (i, 0)),
                  pl.BlockSpec((1, d), lambda i: (0, 0))],
        out_specs=pl.BlockSpec((tm, d), lambda i: (i, 0)),
        out_shape=jax.ShapeDtypeStruct((s, d), F32),
        compiler_params=_params(("parallel",)),
        name="norm_final",
    )(x, r, g)


def _chunk_scan(x, pos, combine, fill):
    sh = 1
    while sh < CHUNK:
        prev = jnp.where(pos >= sh, pltpu.roll(x, sh, 0), fill)
        x = combine(x, prev)
        sh *= 2
    return x


def _gates_kernel(a_ref, w_ref, bias_ref, li_ref, b_ref, mx_ref, lirow_ref, brow_ref):
    tm = a_ref.shape[0]
    pre = jnp.dot(a_ref[...], w_ref[...], preferred_element_type=F32) + bias_ref[...]
    pre = GATE_CAP * jnp.tanh(pre / GATE_CAP)
    logf = jnp.minimum(pre, 0.0) - jnp.log1p(jnp.exp(-jnp.abs(pre)))
    pos = lax.broadcasted_iota(jnp.int32, (tm, LANES), 0) % CHUNK
    b = _chunk_scan(logf, pos, jnp.add, 0.0)
    b = pltpu.roll(b, LANES - N_HEADS, 1)
    li = pre
    mx = b + _chunk_scan(li - b, pos, jnp.maximum, -jnp.inf)
    li_ref[...] = li
    b_ref[...] = b
    mx_ref[...] = mx
    for c in range(tm // CHUNK):
        rows = slice(c * CHUNK, (c + 1) * CHUNK)
        lirow_ref[c] = li[rows].T[:N_HEADS]
        brow_ref[c] = b[rows].T[:N_HEADS]


def _gates(hn, w_g, bias, tm=1024):
    m, k = hn.shape
    col = jax.ShapeDtypeStruct((m, LANES), F32)
    row = jax.ShapeDtypeStruct((m // CHUNK, N_HEADS, CHUNK), F32)
    col_spec = pl.BlockSpec((tm, LANES), lambda i: (i, 0))
    row_spec = pl.BlockSpec((tm // CHUNK, N_HEADS, CHUNK), lambda i: (i, 0, 0))
    return pl.pallas_call(
        _gates_kernel,
        grid=(m // tm,),
        in_specs=[
            pl.BlockSpec((tm, k), lambda i: (i, 0)),
            pl.BlockSpec((k, LANES), lambda i: (0, 0)),
            pl.BlockSpec((1, LANES), lambda i: (0, 0)),
        ],
        out_specs=[col_spec, col_spec, col_spec, row_spec, row_spec],
        out_shape=[col, col, col, row, row],
        compiler_params=_params(("parallel",)),
        name="gates",
    )(hn, w_g, bias)


def _pool_kernel(u_ref, w_ref, scale_ref, o_ref, ext_ref):
    ts = u_ref.shape[0]
    i = pl.program_id(0)

    @pl.when(i == 0)
    def _():
        ext_ref[0:POOL_HALO, :] = jnp.zeros((POOL_HALO, D_POOL), F32)

    @pl.when(i != 0)
    def _():
        ext_ref[0:POOL_HALO, :] = ext_ref[ts:ts + POOL_HALO, :]

    ext_ref[POOL_HALO:POOL_HALO + ts, :] = u_ref[...]
    t = i * ts + lax.broadcasted_iota(jnp.int32, (ts, 1), 0)
    for g, win in enumerate(POOL_WINDOWS):
        cols = slice(g * POOL_GROUP, (g + 1) * POOL_GROUP)
        u = ext_ref[POOL_HALO:POOL_HALO + ts, cols]
        win_sum = u
        for j in range(1, win):
            win_sum = win_sum + ext_ref[POOL_HALO - j:POOL_HALO - j + ts, cols]
        cnt = jnp.minimum(t + 1, win).astype(F32)
        z = (win_sum / cnt - u).astype(BF16)
        y = jnp.dot(z, w_ref[g], preferred_element_type=F32)
        o_ref[:, cols] = (y * scale_ref[:, cols]).astype(o_ref.dtype)


def _pool(proj, w_pool, scale, ts=512):
    s = proj.shape[0]
    return pl.pallas_call(
        _pool_kernel,
        grid=(s // ts,),
        in_specs=[
            pl.BlockSpec((ts, D_POOL), lambda i: (i, 0)),
            pl.BlockSpec((len(POOL_WINDOWS), POOL_GROUP, POOL_GROUP), lambda i: (0, 0, 0)),
            pl.BlockSpec((1, D_POOL), lambda i: (0, 0)),
        ],
        out_specs=pl.BlockSpec((ts, D_POOL), lambda i: (i, 0)),
        out_shape=jax.ShapeDtypeStruct((s, D_POOL), BF16),
        scratch_shapes=[pltpu.VMEM((ts + POOL_HALO, D_POOL), F32)],
        compiler_params=_params(("arbitrary",)),
        name="pool_mixer",
    )(proj, w_pool, scale)


def _mlstm_kernel(qk_ref, v_ref, o_ref, li_ref, b_ref, mx_ref, lirow_ref, brow_ref,
                  cw_ref, cb_ref, gh_ref, y_ref, ext_ref, c_ref, n_ref, m_ref):
    L = CHUNK
    i = pl.program_id(0)

    @pl.when(i == 0)
    def _():
        ext_ref[0:CONV_HALO, :] = jnp.zeros((CONV_HALO, D_QK), F32)
        c_ref[...] = jnp.zeros_like(c_ref)
        n_ref[...] = jnp.zeros_like(n_ref)
        m_ref[...] = jnp.zeros_like(m_ref)

    @pl.when(i != 0)
    def _():
        ext_ref[0:CONV_HALO, :] = ext_ref[L:L + CONV_HALO, :]

    ext_ref[CONV_HALO:CONV_HALO + L, :] = qk_ref[...]

    t_idx = lax.broadcasted_iota(jnp.int32, (L, L), 0)
    s_idx = lax.broadcasted_iota(jnp.int32, (L, L), 1)
    causal = s_idx <= t_idx

    def conv_silu(cols):
        y = cb_ref[:, cols]
        for j in range(CONV_W):
            off = CONV_HALO - (CONV_W - 1) + j
            y = y + cw_ref[j:j + 1, cols] * ext_ref[off:off + L, cols]
        return y * jax.nn.sigmoid(y)

    for h in range(N_HEADS):
        q = conv_silu(slice(h * DQK, (h + 1) * DQK))
        k = conv_silu(slice(N_HEADS * DQK + h * DQK, N_HEADS * DQK + (h + 1) * DQK)) * (DQK ** -0.5)
        vcols = slice(h * DV, (h + 1) * DV)
        v = v_ref[:, vcols].astype(BF16)
        li_col = li_ref[:, h:h + 1]
        b_col = b_ref[:, h:h + 1]
        mx_col = mx_ref[:, h:h + 1]
        li_row = lirow_ref[0, h:h + 1, :]
        b_row = brow_ref[0, h:h + 1, :]
        m_prev = m_ref[h][:, 0:1]
        c_prev = c_ref[h]
        n_prev = n_ref[h]

        qb = q.astype(BF16)
        kb = k.astype(BF16)
        d_mat = jnp.where(causal, b_col - b_row + li_row, -jnp.inf)
        e_col = b_col + m_prev
        m_t = jnp.maximum(mx_col, e_col)
        s_qk = lax.dot_general(qb, kb, (((1,), (1,)), ((), ())), preferred_element_type=F32)
        w_mat = jnp.exp(d_mat - m_t) * s_qk
        w_inter = jnp.exp(e_col - m_t)
        num = jnp.dot(w_mat.astype(BF16), v, preferred_element_type=F32)
        num = num + w_inter * jnp.dot(qb, c_prev.astype(BF16), preferred_element_type=F32)
        den = jnp.sum(w_mat, axis=-1, keepdims=True) + w_inter * jnp.sum(q * n_prev, axis=-1, keepdims=True)
        hh = num / jnp.maximum(jnp.abs(den), jnp.exp(-m_t))
        hh = hh * lax.rsqrt(jnp.mean(hh * hh, axis=-1, keepdims=True) + EPS)
        hh = hh * gh_ref[:, vcols] * jax.nn.sigmoid(o_ref[:, vcols])
        y_ref[:, vcols] = hh.astype(y_ref.dtype)

        g_tot = b_col[L - 1:L, :]
        a_col = g_tot + li_col - b_col
        m_loc = jnp.max(a_col, axis=0, keepdims=True)
        kw = k * jnp.exp(a_col - m_loc)
        c_loc = lax.dot_general(kw.astype(BF16), v, (((0,), (0,)), ((), ())), preferred_element_type=F32)
        n_loc = jnp.sum(kw, axis=0, keepdims=True)
        m_new = jnp.maximum(g_tot + m_prev, m_loc)
        s_old = jnp.exp(g_tot + m_prev - m_new)
        s_loc = jnp.exp(m_loc - m_new)
        c_ref[h] = s_old * c_prev + s_loc * c_loc
        n_ref[h] = s_old * n_prev + s_loc * n_loc
        m_ref[h] = jnp.broadcast_to(m_new, (1, LANES))


def _mlstm(proj, li, b, mx, lirow, brow, conv_w, conv_b, g_head):
    s = proj.shape[0]
    L = CHUNK
    col_spec = pl.BlockSpec((L, LANES), lambda i: (i, 0))
    row_spec = pl.BlockSpec((1, N_HEADS, L), lambda i: (i, 0, 0))
    return pl.pallas_call(
        _mlstm_kernel,
        grid=(s // L,),
        in_specs=[
            pl.BlockSpec((L, D_QK), lambda i: (i, D_POOL // D_QK)),
            pl.BlockSpec((L, D_MLSTM), lambda i: (i, (D_POOL + D_QK) // D_MLSTM)),
            pl.BlockSpec((L, D_MLSTM), lambda i: (i, (D_POOL + D_QK + D_MLSTM) // D_MLSTM)),
            col_spec, col_spec, col_spec, row_spec, row_spec,
            pl.BlockSpec((CONV_W, D_QK), lambda i: (0, 0)),
            pl.BlockSpec((1, D_QK), lambda i: (0, 0)),
            pl.BlockSpec((1, D_MLSTM), lambda i: (0, 0)),
        ],
        out_specs=pl.BlockSpec((L, D_MLSTM), lambda i: (i, 0)),
        out_shape=jax.ShapeDtypeStruct((s, D_MLSTM), BF16),
        scratch_shapes=[
            pltpu.VMEM((L + CONV_HALO, D_QK), F32),
            pltpu.VMEM((N_HEADS, DQK, DV), F32),
            pltpu.VMEM((N_HEADS, 1, DQK), F32),
            pltpu.VMEM((N_HEADS, 1, LANES), F32),
        ],
        compiler_params=_params(("arbitrary",)),
        name="mlstm_mixer",
    )(proj, proj, proj, li, b, mx, lirow, brow, conv_w, conv_b, g_head)


def kernel(x, p, g_mix, w_in, conv_w, conv_b, b_igate, b_fgate, g_head, w_pool, pool_scale,
           w_out, g_mlp, w_up, w_down, g_ple, w_ple_gate, b_ple_gate, w_ple, g_final):
    depth = w_in.shape[0]
    h = x.reshape(SEQ, D_MODEL)
    for l in range(depth):
        w_g = jnp.pad(w_in[l, :, D_MAIN:], ((0, 0), (0, LANES - 2 * N_HEADS))).astype(BF16)
        gate_bias = jnp.pad(jnp.concatenate([b_igate[l], b_fgate[l]]), (0, LANES - 2 * N_HEADS)).reshape(1, LANES)

        hn = _rmsnorm(h, g_mix[l], BF16, "norm_mix")
        proj = _proj_in(l, hn, w_in)
        li, b, mx, lirow, brow = _gates(hn, w_g, gate_bias.astype(F32))
        y_pool = _pool(proj, w_pool[l].astype(BF16), pool_scale[l].reshape(1, D_POOL))
        y_mlstm = _mlstm(proj, li, b, mx, lirow, brow, conv_w[l], conv_b[l].reshape(1, D_QK),
                         g_head[l].reshape(1, D_MLSTM))
        h, hb, ssq = _mix_out(l, y_pool, y_mlstm, w_out, h)

        r = _row_scale(ssq, "scale_mlp")
        act, w_down_b = _mlp_up(l, hb, w_up, g_mlp[l].reshape(D_MODEL, 1), r, w_down)
        h, hb, ssq = _down(act, w_down_b, h)

        r = _row_scale(ssq, "scale_ple")
        h, ssq = _ple(l, hb, w_ple_gate, g_ple[l].reshape(D_MODEL, 1), r, b_ple_gate[l].reshape(1, D_MODEL),
                      p[l].reshape(SEQ, D_PLE).astype(BF16), w_ple[l].astype(BF16), h)
    out = _final_norm(h, _row_scale(ssq, "scale_final"), g_final.reshape(1, D_MODEL))
    return out.reshape(x.shape)
```

```python
import jax
import jax.numpy as jnp
from jax import lax
from jax.experimental import pallas as pl
from jax.experimental.pallas import tpu as pltpu

F32 = jnp.float32
BF16 = jnp.bfloat16

D_MODEL = 4096
SEQ = 8192
D_POOL = 2048
POOL_WINDOWS = (2, 4, 8, 16)
POOL_GROUP = 512
POOL_HALO = 16
D_MLSTM = 2048
N_HEADS = 8
DV = 256
DQK = 128
D_QK = 2 * N_HEADS * DQK
CONV_W = 4
CONV_HALO = 8
GATE_CAP = 15.0
D_FF = 4 * D_MODEL
D_PLE = 256
EPS = 1e-6
D_MAIN = D_POOL + D_QK + 2 * D_MLSTM
LANES = 128
CHUNK = 128
VMEM_LIMIT = 58 * 1024 * 1024


def _params(sem, vmem=VMEM_LIMIT):
    return pltpu.CompilerParams(dimension_semantics=sem, vmem_limit_bytes=vmem)


def _rmsnorm_kernel(x_ref, g_ref, o_ref):
    x = x_ref[...]
    ms = jnp.mean(x * x, axis=-1, keepdims=True)
    o_ref[...] = (x * lax.rsqrt(ms + EPS) * g_ref[...]).astype(o_ref.dtype)


def _rmsnorm(x, g, out_dtype, name, tm=256):
    s, d = x.shape
    return pl.pallas_call(
        _rmsnorm_kernel,
        grid=(s // tm,),
        in_specs=[pl.BlockSpec((tm, d), lambda i: (i, 0)), pl.BlockSpec((1, d), lambda i: (0, 0))],
        out_specs=pl.BlockSpec((tm, d), lambda i: (i, 0)),
        out_shape=jax.ShapeDtypeStruct((s, d), out_dtype),
        compiler_params=_params(("parallel",)),
        name=name,
    )(x, g.reshape(1, d).astype(F32))


def _ws_maps(nb, n_i):
    row = lambda b, i: jnp.where(b > 0, i, 0)
    blk = lambda b: jnp.maximum(b - 1, 0)
    wrow = lambda b, i: jnp.where(b < nb, i, n_i - 1)
    wblk = lambda b: jnp.minimum(b, nb - 1)
    return row, blk, wrow, wblk


def _ws_step(wchunk_ref, g_ref, wbufs, compute, transposed=False):
    b = pl.program_id(0)
    i = pl.program_id(1)

    def stage(buf):
        w = wchunk_ref[...]
        if transposed:
            w = w.T
        if g_ref is not None:
            w = w * g_ref[...]
        ck = w.shape[0]
        buf[pl.ds(pl.multiple_of(i * ck, ck), ck), :] = w.astype(BF16)

    @pl.when(b == 0)
    def _():
        stage(wbufs[0])

    for parity in (0, 1):
        @pl.when(jnp.logical_and(b > 0, b % 2 == parity))
        def _():
            stage(wbufs[parity])
            compute(wbufs[1 - parity])


def _row_sumsq(h):
    return jnp.broadcast_to(jnp.sum(h * h, axis=1, keepdims=True), (h.shape[0], LANES))


def _lane_tile(r, n):
    return jnp.concatenate([r] * (n // LANES), axis=1)


def _proj_kernel(a_ref, wchunk_ref, o_ref, wbuf0, wbuf1):
    def compute(w_ref):
        o_ref[...] = jnp.dot(a_ref[...], w_ref[...], preferred_element_type=F32)
    _ws_step(wchunk_ref, None, (wbuf0, wbuf1), compute, transposed=True)


def _mix_out_kernel(yp_ref, ym_ref, wchunk_ref, res_ref, o_ref, ob_ref, ssq_ref, wbuf0, wbuf1):
    def compute(w_ref):
        acc = jnp.dot(yp_ref[...], w_ref[0:D_POOL, :], preferred_element_type=F32)
        acc += jnp.dot(ym_ref[...], w_ref[D_POOL:D_POOL + D_MLSTM, :], preferred_element_type=F32)
        h = res_ref[...] + acc
        o_ref[...] = h
        ob_ref[...] = h.astype(BF16)
        ssq_ref[...] = _row_sumsq(h)
    _ws_step(wchunk_ref, None, (wbuf0, wbuf1), compute)


def _up_kernel(a_ref, wchunk_ref, g_ref, r_ref, wd_ref, o_ref, wdb_ref, wbuf0, wbuf1):
    def compute(w_ref):
        y = jnp.dot(a_ref[...], w_ref[...], preferred_element_type=F32)
        y = jnp.maximum(y * _lane_tile(r_ref[...], y.shape[1]), 0.0)
        o_ref[...] = (y * y).astype(BF16)
        wdb_ref[...] = wd_ref[...].astype(BF16)
    _ws_step(wchunk_ref, g_ref, (wbuf0, wbuf1), compute)


def _ple_kernel(a_ref, wchunk_ref, g_ref, r_ref, bias_ref, p_ref, wp_ref, res_ref, o_ref, ssq_ref, wbuf0, wbuf1):
    def compute(w_ref):
        z = jnp.dot(a_ref[...], w_ref[...], preferred_element_type=F32)
        z = z * _lane_tile(r_ref[...], z.shape[1]) + bias_ref[...]
        pe = jnp.dot(p_ref[...], wp_ref[...], preferred_element_type=F32)
        h = res_ref[...] + jax.nn.sigmoid(z) * pe
        o_ref[...] = h
        ssq_ref[...] = _row_sumsq(h)
    _ws_step(wchunk_ref, g_ref, (wbuf0, wbuf1), compute)


def _ws_call(kern, name, l, a_list, w, k_dim, n_out, tm, tn, extra_in, extra_specs, out_shapes, out_specs,
             transposed=False):
    m = a_list[0].shape[0]
    nb, n_i = n_out // tn, m // tm
    ck = k_dim // n_i
    row, blk, wrow, wblk = _ws_maps(nb, n_i)
    a_specs = [pl.BlockSpec((tm, a.shape[1]), lambda b, i: (row(b, i), 0)) for a in a_list]
    if transposed:
        w_spec = pl.BlockSpec((None, tn, ck), lambda b, i: (l, wblk(b), wrow(b, i)))
    else:
        w_spec = pl.BlockSpec((None, ck, tn), lambda b, i: (l, wrow(b, i), wblk(b)))
    return pl.pallas_call(
        kern,
        grid=(nb + 1, n_i),
        in_specs=a_specs + [w_spec] + extra_specs(row, blk, wrow, wblk),
        out_specs=out_specs(row, blk),
        out_shape=out_shapes,
        scratch_shapes=[pltpu.VMEM((k_dim, tn), BF16), pltpu.VMEM((k_dim, tn), BF16)],
        compiler_params=_params(("arbitrary", "arbitrary")),
        name=name,
    )(*a_list, w, *extra_in)


def _proj_in(l, hn, w_in_t, tm=1024, tn=1024):
    m = hn.shape[0]
    return _ws_call(
        _proj_kernel, "proj_in", l, [hn], w_in_t, D_MODEL, D_MAIN, tm, tn, [],
        lambda row, blk, wrow, wblk: [],
        jax.ShapeDtypeStruct((m, D_MAIN), F32),
        lambda row, blk: pl.BlockSpec((tm, tn), lambda b, i: (row(b, i), blk(b))),
        transposed=True)


def _mix_out(l, y_pool, y_mlstm, w_out, resid, tm=512, tn=1024):
    m = resid.shape[0]
    nb = D_MODEL // tn
    tile = lambda row, blk: pl.BlockSpec((tm, tn), lambda b, i: (row(b, i), blk(b)))
    return _ws_call(
        _mix_out_kernel, "mix_out", l, [y_pool, y_mlstm], w_out, D_MODEL, D_MODEL, tm, tn, [resid],
        lambda row, blk, wrow, wblk: [tile(row, blk)],
        [jax.ShapeDtypeStruct((m, D_MODEL), F32), jax.ShapeDtypeStruct((m, D_MODEL), BF16),
         jax.ShapeDtypeStruct((m, nb * LANES), F32)],
        lambda row, blk: [tile(row, blk), tile(row, blk),
                          pl.BlockSpec((tm, LANES), lambda b, i: (row(b, i), blk(b)))])


def _mlp_up(l, hb, w_up, g, r, w_down, tm=1024, tn=1024):
    m = hb.shape[0]
    nb, n_i = D_FF // tn, m // tm
    ck = D_MODEL // n_i
    wd_rows = D_FF // (nb * n_i)
    side = lambda row, blk: (lambda b, i: (blk(b) * n_i + row(b, i), 0))
    return _ws_call(
        _up_kernel, "mlp_up", l, [hb], w_up, D_MODEL, D_FF, tm, tn, [g, r, w_down],
        lambda row, blk, wrow, wblk: [
            pl.BlockSpec((ck, 1), lambda b, i: (wrow(b, i), 0)),
            pl.BlockSpec((tm, LANES), lambda b, i: (row(b, i), 0)),
            pl.BlockSpec((None, wd_rows, D_MODEL), lambda b, i: (l,) + side(row, blk)(b, i)),
        ],
        [jax.ShapeDtypeStruct((m, D_FF), BF16), jax.ShapeDtypeStruct((D_FF, D_MODEL), BF16)],
        lambda row, blk: [pl.BlockSpec((tm, tn), lambda b, i: (row(b, i), blk(b))),
                          pl.BlockSpec((wd_rows, D_MODEL), side(row, blk))])


def _ple(l, hb, w_gate, g, r, b_gate, p, w_ple, resid, tm=512, tn=1024):
    m = hb.shape[0]
    nb, n_i = D_MODEL // tn, m // tm
    ck = D_MODEL // n_i
    tile = lambda row, blk: pl.BlockSpec((tm, tn), lambda b, i: (row(b, i), blk(b)))
    return _ws_call(
        _ple_kernel, "ple", l, [hb], w_gate, D_MODEL, D_MODEL, tm, tn, [g, r, b_gate, p, w_ple, resid],
        lambda row, blk, wrow, wblk: [
            pl.BlockSpec((ck, 1), lambda b, i: (wrow(b, i), 0)),
            pl.BlockSpec((tm, LANES), lambda b, i: (row(b, i), 0)),
            pl.BlockSpec((1, tn), lambda b, i: (0, blk(b))),
            pl.BlockSpec((tm, D_PLE), lambda b, i: (row(b, i), 0)),
            pl.BlockSpec((D_PLE, tn), lambda b, i: (0, blk(b))),
            tile(row, blk),
        ],
        [jax.ShapeDtypeStruct((m, D_MODEL), F32), jax.ShapeDtypeStruct((m, nb * LANES), F32)],
        lambda row, blk: [tile(row, blk), pl.BlockSpec((tm, LANES), lambda b, i: (row(b, i), blk(b)))])


def _down_kernel(a_ref, w_ref, r_ref, o_ref, ob_ref, ssq_ref):
    k = pl.program_id(2)
    last = pl.num_programs(2) - 1

    def part():
        return jnp.dot(a_ref[...], w_ref[...], preferred_element_type=F32)

    @pl.when(k == 0)
    def _():
        o_ref[...] = r_ref[...] + part()

    @pl.when(jnp.logical_and(k != 0, k != last))
    def _():
        o_ref[...] += part()

    @pl.when(k == last)
    def _():
        h = o_ref[...] + part()
        o_ref[...] = h
        ob_ref[...] = h.astype(BF16)
        ssq_ref[...] = _row_sumsq(h)


def _down(a, w, resid, tm=1024, tn=1024, tk=2048):
    m, k = a.shape
    n = w.shape[1]
    tile = pl.BlockSpec((tm, tn), lambda i, j, kk: (i, j))
    return pl.pallas_call(
        _down_kernel,
        grid=(m // tm, n // tn, k // tk),
        in_specs=[
            pl.BlockSpec((tm, tk), lambda i, j, kk: (i, kk)),
            pl.BlockSpec((tk, tn), lambda i, j, kk: (kk, j)),
            tile,
        ],
        out_specs=[tile, tile, pl.BlockSpec((tm, LANES), lambda i, j, kk: (i, j))],
        out_shape=[jax.ShapeDtypeStruct((m, n), F32), jax.ShapeDtypeStruct((m, n), BF16),
                   jax.ShapeDtypeStruct((m, (n // tn) * LANES), F32)],
        compiler_params=_params(("parallel", "arbitrary", "arbitrary")),
        name="mlp_down",
    )(a, w, resid)


def _row_scale_kernel(ssq_ref, r_ref):
    s = ssq_ref[:, 0:LANES]
    for j in range(1, ssq_ref.shape[1] // LANES):
        s = s + ssq_ref[:, j * LANES:(j + 1) * LANES]
    r_ref[...] = lax.rsqrt(s * (1.0 / D_MODEL) + EPS)


def _row_scale(ssq, name, tm=1024):
    m, w = ssq.shape
    return pl.pallas_call(
        _row_scale_kernel,
        grid=(m // tm,),
        in_specs=[pl.BlockSpec((tm, w), lambda i: (i, 0))],
        out_specs=pl.BlockSpec((tm, LANES), lambda i: (i, 0)),
        out_shape=jax.ShapeDtypeStruct((m, LANES), F32),
        compiler_params=_params(("parallel",)),
        name=name,
    )(ssq)


def _scale_kernel(x_ref, r_ref, g_ref, o_ref):
    o_ref[...] = x_ref[...] * r_ref[:, 0:1] * g_ref[...]


def _final_norm(x, r, g, tm=256):
    s, d = x.shape
    return pl.pallas_call(
        _scale_kernel,
        grid=(s // tm,),
        in_specs=[pl.BlockSpec((tm, d), lambda i: (i, 0)), pl.BlockSpec((tm, LANES), lambda i: (i, 0)),
                  pl.BlockSpec((1, d), lambda i: (0, 0))],
        out_specs=pl.BlockSpec((tm, d), lambda i: (i, 0)),
        out_shape=jax.ShapeDtypeStruct((s, d), F32),
        compiler_params=_params(("parallel",)),
        name="norm_final",
    )(x, r, g)


def _chunk_scan(x, pos, combine, fill):
    sh = 1
    while sh < CHUNK:
        prev = jnp.where(pos >= sh, pltpu.roll(x, sh, 0), fill)
        x = combine(x, prev)
        sh *= 2
    return x


def _gates_kernel(a_ref, w_ref, bias_ref, li_ref, b_ref, mx_ref, lirow_ref, brow_ref):
    tm = a_ref.shape[0]
    pre = jnp.dot(a_ref[...], w_ref[...], preferred_element_type=F32) + bias_ref[...]
    pre = GATE_CAP * jnp.tanh(pre / GATE_CAP)
    logf = jnp.minimum(pre, 0.0) - jnp.log1p(jnp.exp(-jnp.abs(pre)))
    pos = lax.broadcasted_iota(jnp.int32, (tm, LANES), 0) % CHUNK
    b = _chunk_scan(logf, pos, jnp.add, 0.0)
    b = pltpu.roll(b, LANES - N_HEADS, 1)
    li = pre
    mx = b + _chunk_scan(li - b, pos, jnp.maximum, -jnp.inf)
    li_ref[...] = li
    b_ref[...] = b
    mx_ref[...] = mx
    for c in range(tm // CHUNK):
        rows = slice(c * CHUNK, (c + 1) * CHUNK)
        lirow_ref[c] = li[rows].T[:N_HEADS]
        brow_ref[c] = b[rows].T[:N_HEADS]


def _gates(hn, w_g, bias, tm=1024):
    m, k = hn.shape
    col = jax.ShapeDtypeStruct((m, LANES), F32)
    row = jax.ShapeDtypeStruct((m // CHUNK, N_HEADS, CHUNK), F32)
    col_spec = pl.BlockSpec((tm, LANES), lambda i: (i, 0))
    row_spec = pl.BlockSpec((tm // CHUNK, N_HEADS, CHUNK), lambda i: (i, 0, 0))
    return pl.pallas_call(
        _gates_kernel,
        grid=(m // tm,),
        in_specs=[
            pl.BlockSpec((tm, k), lambda i: (i, 0)),
            pl.BlockSpec((k, LANES), lambda i: (0, 0)),
            pl.BlockSpec((1, LANES), lambda i: (0, 0)),
        ],
        out_specs=[col_spec, col_spec, col_spec, row_spec, row_spec],
        out_shape=[col, col, col, row, row],
        compiler_params=_params(("parallel",)),
        name="gates",
    )(hn, w_g, bias)


def _pool_kernel(u_ref, w_ref, scale_ref, o_ref, ext_ref):
    ts = u_ref.shape[0]
    i = pl.program_id(0)

    @pl.when(i == 0)
    def _():
        ext_ref[0:POOL_HALO, :] = jnp.zeros((POOL_HALO, D_POOL), F32)

    @pl.when(i != 0)
    def _():
        ext_ref[0:POOL_HALO, :] = ext_ref[ts:ts + POOL_HALO, :]

    ext_ref[POOL_HALO:POOL_HALO + ts, :] = u_ref[...]
    t = i * ts + lax.broadcasted_iota(jnp.int32, (ts, 1), 0)
    for g, win in enumerate(POOL_WINDOWS):
        cols = slice(g * POOL_GROUP, (g + 1) * POOL_GROUP)
        u = ext_ref[POOL_HALO:POOL_HALO + ts, cols]
        win_sum = u
        for j in range(1, win):
            win_sum = win_sum + ext_ref[POOL_HALO - j:POOL_HALO - j + ts, cols]
        cnt = jnp.minimum(t + 1, win).astype(F32)
        z = (win_sum / cnt - u).astype(BF16)
        y = jnp.dot(z, w_ref[g], preferred_element_type=F32)
        o_ref[:, cols] = (y * scale_ref[:, cols]).astype(o_ref.dtype)


def _pool(proj, w_pool, scale, ts=512):
    s = proj.shape[0]
    return pl.pallas_call(
        _pool_kernel,
        grid=(s // ts,),
        in_specs=[
            pl.BlockSpec((ts, D_POOL), lambda i: (i, 0)),
            pl.BlockSpec((len(POOL_WINDOWS), POOL_GROUP, POOL_GROUP), lambda i: (0, 0, 0)),
            pl.BlockSpec((1, D_POOL), lambda i: (0, 0)),
        ],
        out_specs=pl.BlockSpec((ts, D_POOL), lambda i: (i, 0)),
        out_shape=jax.ShapeDtypeStruct((s, D_POOL), BF16),
        scratch_shapes=[pltpu.VMEM((ts + POOL_HALO, D_POOL), F32)],
        compiler_params=_params(("arbitrary",)),
        name="pool_mixer",
    )(proj, w_pool, scale)


def _mlstm_kernel(qk_ref, v_ref, o_ref, li_ref, b_ref, mx_ref, lirow_ref, brow_ref,
                  cw_ref, cb_ref, gh_ref, y_ref, ext_ref, c_ref, n_ref, m_ref):
    L = CHUNK
    i = pl.program_id(0)

    @pl.when(i == 0)
    def _():
        ext_ref[0:CONV_HALO, :] = jnp.zeros((CONV_HALO, D_QK), F32)
        c_ref[...] = jnp.zeros_like(c_ref)
        n_ref[...] = jnp.zeros_like(n_ref)
        m_ref[...] = jnp.zeros_like(m_ref)

    @pl.when(i != 0)
    def _():
        ext_ref[0:CONV_HALO, :] = ext_ref[L:L + CONV_HALO, :]

    ext_ref[CONV_HALO:CONV_HALO + L, :] = qk_ref[...]

    t_idx = lax.broadcasted_iota(jnp.int32, (L, L), 0)
    s_idx = lax.broadcasted_iota(jnp.int32, (L, L), 1)
    causal = s_idx <= t_idx

    def conv_silu(cols):
        y = cb_ref[:, cols]
        for j in range(CONV_W):
            off = CONV_HALO - (CONV_W - 1) + j
            y = y + cw_ref[j:j + 1, cols] * ext_ref[off:off + L, cols]
        return y * jax.nn.sigmoid(y)

    for h in range(N_HEADS):
        q = conv_silu(slice(h * DQK, (h + 1) * DQK))
        k = conv_silu(slice(N_HEADS * DQK + h * DQK, N_HEADS * DQK + (h + 1) * DQK)) * (DQK ** -0.5)
        vcols = slice(h * DV, (h + 1) * DV)
        v = v_ref[:, vcols].astype(BF16)
        li_col = li_ref[:, h:h + 1]
        b_col = b_ref[:, h:h + 1]
        mx_col = mx_ref[:, h:h + 1]
        li_row = lirow_ref[0, h:h + 1, :]
        b_row = brow_ref[0, h:h + 1, :]
        m_prev = m_ref[h][:, 0:1]
        c_prev = c_ref[h]
        n_prev = n_ref[h]

        qb = q.astype(BF16)
        kb = k.astype(BF16)
        d_mat = jnp.where(causal, b_col - b_row + li_row, -jnp.inf)
        e_col = b_col + m_prev
        m_t = jnp.maximum(mx_col, e_col)
        s_qk = lax.dot_general(qb, kb, (((1,), (1,)), ((), ())), preferred_element_type=F32)
        w_mat = jnp.exp(d_mat - m_t) * s_qk
        w_inter = jnp.exp(e_col - m_t)
        num = jnp.dot(w_mat.astype(BF16), v, preferred_element_type=F32)
        num = num + w_inter * jnp.dot(qb, c_prev.astype(BF16), preferred_element_type=F32)
        den = jnp.sum(w_mat, axis=-1, keepdims=True) + w_inter * jnp.sum(q * n_prev, axis=-1, keepdims=True)
        hh = num / jnp.maximum(jnp.abs(den), jnp.exp(-m_t))
        hh = hh * lax.rsqrt(jnp.mean(hh * hh, axis=-1, keepdims=True) + EPS)
        hh = hh * gh_ref[:, vcols] * jax.nn.sigmoid(o_ref[:, vcols])
        y_ref[:, vcols] = hh.astype(y_ref.dtype)

        g_tot = b_col[L - 1:L, :]
        a_col = g_tot + li_col - b_col
        m_loc = jnp.max(a_col, axis=0, keepdims=True)
        kw = k * jnp.exp(a_col - m_loc)
        c_loc = lax.dot_general(kw.astype(BF16), v, (((0,), (0,)), ((), ())), preferred_element_type=F32)
        n_loc = jnp.sum(kw, axis=0, keepdims=True)
        m_new = jnp.maximum(g_tot + m_prev, m_loc)
        s_old = jnp.exp(g_tot + m_prev - m_new)
        s_loc = jnp.exp(m_loc - m_new)
        c_ref[h] = s_old * c_prev + s_loc * c_loc
        n_ref[h] = s_old * n_prev + s_loc * n_loc
        m_ref[h] = jnp.broadcast_to(m_new, (1, LANES))


def _mlstm(proj, li, b, mx, lirow, brow, conv_w, conv_b, g_head):
    s = proj.shape[0]
    L = CHUNK
    col_spec = pl.BlockSpec((L, LANES), lambda i: (i, 0))
    row_spec = pl.BlockSpec((1, N_HEADS, L), lambda i: (i, 0, 0))
    return pl.pallas_call(
        _mlstm_kernel,
        grid=(s // L,),
        in_specs=[
            pl.BlockSpec((L, D_QK), lambda i: (i, D_POOL // D_QK)),
            pl.BlockSpec((L, D_MLSTM), lambda i: (i, (D_POOL + D_QK) // D_MLSTM)),
            pl.BlockSpec((L, D_MLSTM), lambda i: (i, (D_POOL + D_QK + D_MLSTM) // D_MLSTM)),
            col_spec, col_spec, col_spec, row_spec, row_spec,
            pl.BlockSpec((CONV_W, D_QK), lambda i: (0, 0)),
            pl.BlockSpec((1, D_QK), lambda i: (0, 0)),
            pl.BlockSpec((1, D_MLSTM), lambda i: (0, 0)),
        ],
        out_specs=pl.BlockSpec((L, D_MLSTM), lambda i: (i, 0)),
        out_shape=jax.ShapeDtypeStruct((s, D_MLSTM), BF16),
        scratch_shapes=[
            pltpu.VMEM((L + CONV_HALO, D_QK), F32),
            pltpu.VMEM((N_HEADS, DQK, DV), F32),
            pltpu.VMEM((N_HEADS, 1, DQK), F32),
            pltpu.VMEM((N_HEADS, 1, LANES), F32),
        ],
        compiler_params=_params(("arbitrary",)),
        name="mlstm_mixer",
    )(proj, proj, proj, li, b, mx, lirow, brow, conv_w, conv_b, g_head)


def kernel(x, p, g_mix, w_in, conv_w, conv_b, b_igate, b_fgate, g_head, w_pool, pool_scale,
           w_out, g_mlp, w_up, w_down, g_ple, w_ple_gate, b_ple_gate, w_ple, g_final):
    depth = w_in.shape[0]
    h = x.reshape(SEQ, D_MODEL)
    for l in range(depth):
        w_g = jnp.pad(w_in[l, :, D_MAIN:], ((0, 0), (0, LANES - 2 * N_HEADS))).astype(BF16)
        gate_bias = jnp.pad(jnp.concatenate([b_igate[l], b_fgate[l]]), (0, LANES - 2 * N_HEADS)).reshape(1, LANES)

        hn = _rmsnorm(h, g_mix[l], BF16, "norm_mix")
        proj = _proj_in(l, hn, jnp.swapaxes(w_in, 1, 2))
        li, b, mx, lirow, brow = _gates(hn, w_g, gate_bias.astype(F32))
        y_pool = _pool(proj, w_pool[l].astype(BF16), pool_scale[l].reshape(1, D_POOL))
        y_mlstm = _mlstm(proj, li, b, mx, lirow, brow, conv_w[l], conv_b[l].reshape(1, D_QK),
                         g_head[l].reshape(1, D_MLSTM))
        h, hb, ssq = _mix_out(l, y_pool, y_mlstm, w_out, h)

        r = _row_scale(ssq, "scale_mlp")
        act, w_down_b = _mlp_up(l, hb, w_up, g_mlp[l].reshape(D_MODEL, 1), r, w_down)
        h, hb, ssq = _down(act, w_down_b, h)

        r = _row_scale(ssq, "scale_ple")
        h, ssq = _ple(l, hb, w_ple_gate, g_ple[l].reshape(D_MODEL, 1), r, b_ple_gate[l].reshape(1, D_MODEL),
                      p[l].reshape(SEQ, D_PLE).astype(BF16), w_ple[l].astype(BF16), h)
    out = _final_norm(h, _row_scale(ssq, "scale_final"), g_final.reshape(1, D_MODEL))
    return out.reshape(x.shape)
```

```python
import jax
import jax.numpy as jnp
from jax import lax
from jax.experimental import pallas as pl
from jax.experimental.pallas import tpu as pltpu

F32 = jnp.float32
BF16 = jnp.bfloat16

D_MODEL = 4096
SEQ = 8192
D_POOL = 2048
POOL_WINDOWS = (2, 4, 8, 16)
POOL_GROUP = 512
POOL_HALO = 16
D_MLSTM = 2048
N_HEADS = 8
DV = 256
DQK = 128
D_QK = 2 * N_HEADS * DQK
CONV_W = 4
CONV_HALO = 8
GATE_CAP = 15.0
D_FF = 4 * D_MODEL
D_PLE = 256
EPS = 1e-6
D_MAIN = D_POOL + D_QK + 2 * D_MLSTM
LANES = 128
CHUNK = 128
VMEM_LIMIT = 58 * 1024 * 1024


def _params(sem, vmem=VMEM_LIMIT):
    return pltpu.CompilerParams(dimension_semantics=sem, vmem_limit_bytes=vmem)


def _rmsnorm_kernel(x_ref, g_ref, o_ref):
    x = x_ref[...]
    ms = jnp.mean(x * x, axis=-1, keepdims=True)
    o_ref[...] = (x * lax.rsqrt(ms + EPS) * g_ref[...]).astype(o_ref.dtype)


def _rmsnorm(x, g, out_dtype, name, tm=256):
    s, d = x.shape
    return pl.pallas_call(
        _rmsnorm_kernel,
        grid=(s // tm,),
        in_specs=[pl.BlockSpec((tm, d), lambda i: (i, 0)), pl.BlockSpec((1, d), lambda i: (0, 0))],
        out_specs=pl.BlockSpec((tm, d), lambda i: (i, 0)),
        out_shape=jax.ShapeDtypeStruct((s, d), out_dtype),
        compiler_params=_params(("parallel",)),
        name=name,
    )(x, g.reshape(1, d).astype(F32))


def _ws_maps(nb, n_i):
    row = lambda b, i: jnp.where(b > 0, i, 0)
    blk = lambda b: jnp.maximum(b - 1, 0)
    wrow = lambda b, i: jnp.where(b < nb, i, n_i - 1)
    wblk = lambda b: jnp.minimum(b, nb - 1)
    return row, blk, wrow, wblk


def _ws_step(wchunk_ref, g_ref, wbufs, compute, transposed=False):
    b = pl.program_id(0)
    i = pl.program_id(1)

    def stage(buf):
        w = wchunk_ref[...]
        if transposed:
            w = w.T
        if g_ref is not None:
            w = w * g_ref[...]
        ck = w.shape[0]
        buf[pl.ds(pl.multiple_of(i * ck, ck), ck), :] = w.astype(BF16)

    @pl.when(b == 0)
    def _():
        stage(wbufs[0])

    for parity in (0, 1):
        @pl.when(jnp.logical_and(b > 0, b % 2 == parity))
        def _():
            stage(wbufs[parity])
            compute(wbufs[1 - parity])


def _row_sumsq(h):
    return jnp.broadcast_to(jnp.sum(h * h, axis=1, keepdims=True), (h.shape[0], LANES))


def _lane_tile(r, n):
    return jnp.concatenate([r] * (n // LANES), axis=1)


def _proj_kernel(a_ref, wchunk_ref, o_ref, wbuf0, wbuf1):
    def compute(w_ref):
        o_ref[...] = jnp.dot(a_ref[...], w_ref[...], preferred_element_type=F32)
    _ws_step(wchunk_ref, None, (wbuf0, wbuf1), compute, transposed=True)


def _mix_out_kernel(yp_ref, ym_ref, wchunk_ref, res_ref, o_ref, ob_ref, ssq_ref, wbuf0, wbuf1):
    def compute(w_ref):
        acc = jnp.dot(yp_ref[...], w_ref[0:D_POOL, :], preferred_element_type=F32)
        acc += jnp.dot(ym_ref[...], w_ref[D_POOL:D_POOL + D_MLSTM, :], preferred_element_type=F32)
        h = res_ref[...] + acc
        o_ref[...] = h
        ob_ref[...] = h.astype(BF16)
        ssq_ref[...] = _row_sumsq(h)
    _ws_step(wchunk_ref, None, (wbuf0, wbuf1), compute)


def _up_kernel(a_ref, wchunk_ref, g_ref, r_ref, wd_ref, o_ref, wdb_ref, wbuf0, wbuf1):
    def compute(w_ref):
        y = jnp.dot(a_ref[...], w_ref[...], preferred_element_type=F32)
        y = jnp.maximum(y * _lane_tile(r_ref[...], y.shape[1]), 0.0)
        o_ref[...] = (y * y).astype(BF16)
        wdb_ref[...] = wd_ref[...].astype(BF16)
    _ws_step(wchunk_ref, g_ref, (wbuf0, wbuf1), compute)


def _ws_call(kern, name, l, a_list, w, k_dim, n_out, tm, tn, extra_in, extra_specs, out_shapes, out_specs,
             transposed=False):
    m = a_list[0].shape[0]
    nb, n_i = n_out // tn, m // tm
    ck = k_dim // n_i
    row, blk, wrow, wblk = _ws_maps(nb, n_i)
    a_specs = [pl.BlockSpec((tm, a.shape[1]), lambda b, i: (row(b, i), 0)) for a in a_list]
    if transposed:
        w_spec = pl.BlockSpec((None, tn, ck), lambda b, i: (l, wblk(b), wrow(b, i)))
    else:
        w_spec = pl.BlockSpec((None, ck, tn), lambda b, i: (l, wrow(b, i), wblk(b)))
    return pl.pallas_call(
        kern,
        grid=(nb + 1, n_i),
        in_specs=a_specs + [w_spec] + extra_specs(row, blk, wrow, wblk),
        out_specs=out_specs(row, blk),
        out_shape=out_shapes,
        scratch_shapes=[pltpu.VMEM((k_dim, tn), BF16), pltpu.VMEM((k_dim, tn), BF16)],
        compiler_params=_params(("arbitrary", "arbitrary")),
        name=name,
    )(*a_list, w, *extra_in)


def _proj_in(l, hn, w_in_t, tm=1024, tn=1024):
    m = hn.shape[0]
    return _ws_call(
        _proj_kernel, "proj_in", l, [hn], w_in_t, D_MODEL, D_MAIN, tm, tn, [],
        lambda row, blk, wrow, wblk: [],
        jax.ShapeDtypeStruct((m, D_MAIN), F32),
        lambda row, blk: pl.BlockSpec((tm, tn), lambda b, i: (row(b, i), blk(b))),
        transposed=True)


def _mix_out(l, y_pool, y_mlstm, w_out, resid, tm=512, tn=1024):
    m = resid.shape[0]
    nb = D_MODEL // tn
    tile = lambda row, blk: pl.BlockSpec((tm, tn), lambda b, i: (row(b, i), blk(b)))
    return _ws_call(
        _mix_out_kernel, "mix_out", l, [y_pool, y_mlstm], w_out, D_MODEL, D_MODEL, tm, tn, [resid],
        lambda row, blk, wrow, wblk: [tile(row, blk)],
        [jax.ShapeDtypeStruct((m, D_MODEL), F32), jax.ShapeDtypeStruct((m, D_MODEL), BF16),
         jax.ShapeDtypeStruct((m, nb * LANES), F32)],
        lambda row, blk: [tile(row, blk), tile(row, blk),
                          pl.BlockSpec((tm, LANES), lambda b, i: (row(b, i), blk(b)))])


def _mlp_up(l, hb, w_up, g, r, w_down, tm=1024, tn=1024):
    m = hb.shape[0]
    nb, n_i = D_FF // tn, m // tm
    ck = D_MODEL // n_i
    wd_rows = D_FF // (nb * n_i)
    side = lambda row, blk: (lambda b, i: (blk(b) * n_i + row(b, i), 0))
    return _ws_call(
        _up_kernel, "mlp_up", l, [hb], w_up, D_MODEL, D_FF, tm, tn, [g, r, w_down],
        lambda row, blk, wrow, wblk: [
            pl.BlockSpec((ck, 1), lambda b, i: (wrow(b, i), 0)),
            pl.BlockSpec((tm, LANES), lambda b, i: (row(b, i), 0)),
            pl.BlockSpec((None, wd_rows, D_MODEL), lambda b, i: (l,) + side(row, blk)(b, i)),
        ],
        [jax.ShapeDtypeStruct((m, D_FF), BF16), jax.ShapeDtypeStruct((D_FF, D_MODEL), BF16)],
        lambda row, blk: [pl.BlockSpec((tm, tn), lambda b, i: (row(b, i), blk(b))),
                          pl.BlockSpec((wd_rows, D_MODEL), side(row, blk))])


def _ple_kernel(a_ref, w_ref, ssq_ref, bias_ref, p_ref, wp_ref, res_ref, gf_ref, o_ref):
    j = pl.program_id(1)
    tn = w_ref.shape[1]
    n_j = o_ref.shape[1] // tn

    def tile():
        s = ssq_ref[:, 0:LANES]
        for c in range(1, ssq_ref.shape[1] // LANES):
            s = s + ssq_ref[:, c * LANES:(c + 1) * LANES]
        r = lax.rsqrt(s * (1.0 / D_MODEL) + EPS)
        z = jnp.dot(a_ref[...], w_ref[...], preferred_element_type=F32)
        z = z * _lane_tile(r, tn) + bias_ref[...]
        pe = jnp.dot(p_ref[...], wp_ref[...], preferred_element_type=F32)
        return res_ref[...] + jax.nn.sigmoid(z) * pe

    for jj in range(n_j):
        @pl.when(j == jj)
        def _():
            o_ref[:, jj * tn:(jj + 1) * tn] = tile()
            if jj == n_j - 1:
                x = o_ref[...]
                ms = jnp.mean(x * x, axis=-1, keepdims=True)
                o_ref[...] = x * lax.rsqrt(ms + EPS) * gf_ref[...]


def _ple_final(hb, w_gate_b, ssq, b_gate, p, w_ple, resid, g_final, tm=512, tn=1024):
    m = hb.shape[0]
    return pl.pallas_call(
        _ple_kernel,
        grid=(m // tm, D_MODEL // tn),
        in_specs=[
            pl.BlockSpec((tm, D_MODEL), lambda i, j: (i, 0)),
            pl.BlockSpec((D_MODEL, tn), lambda i, j: (0, j)),
            pl.BlockSpec((tm, ssq.shape[1]), lambda i, j: (i, 0)),
            pl.BlockSpec((1, tn), lambda i, j: (0, j)),
            pl.BlockSpec((tm, D_PLE), lambda i, j: (i, 0)),
            pl.BlockSpec((D_PLE, tn), lambda i, j: (0, j)),
            pl.BlockSpec((tm, tn), lambda i, j: (i, j)),
            pl.BlockSpec((1, D_MODEL), lambda i, j: (0, 0)),
        ],
        out_specs=pl.BlockSpec((tm, D_MODEL), lambda i, j: (i, 0)),
        out_shape=jax.ShapeDtypeStruct((m, D_MODEL), F32),
        compiler_params=_params(("parallel", "arbitrary")),
        name="ple_final",
    )(hb, w_gate_b, ssq, b_gate, p, w_ple, resid, g_final)


def _down_kernel(a_ref, w_ref, r_ref, wg_ref, g_ref, o_ref, ob_ref, ssq_ref, wgb_ref):
    k = pl.program_id(2)
    last = pl.num_programs(2) - 1
    wgb_ref[...] = (wg_ref[...] * g_ref[...]).astype(BF16)

    def part():
        return jnp.dot(a_ref[...], w_ref[...], preferred_element_type=F32)

    @pl.when(k == 0)
    def _():
        o_ref[...] = r_ref[...] + part()

    @pl.when(jnp.logical_and(k != 0, k != last))
    def _():
        o_ref[...] += part()

    @pl.when(k == last)
    def _():
        h = o_ref[...] + part()
        o_ref[...] = h
        ob_ref[...] = h.astype(BF16)
        ssq_ref[...] = _row_sumsq(h)


def _down(l, a, w, resid, w_gate, g_gate, tm=1024, tn=1024, tk=2048):
    m, k = a.shape
    n = w.shape[1]
    n_j, n_k = n // tn, k // tk
    side_rows = w_gate.shape[1] // ((m // tm) * n_j * n_k)
    tile = pl.BlockSpec((tm, tn), lambda i, j, kk: (i, j))
    step = lambda i, j, kk: (i * n_j + j) * n_k + kk
    return pl.pallas_call(
        _down_kernel,
        grid=(m // tm, n_j, n_k),
        in_specs=[
            pl.BlockSpec((tm, tk), lambda i, j, kk: (i, kk)),
            pl.BlockSpec((tk, tn), lambda i, j, kk: (kk, j)),
            tile,
            pl.BlockSpec((None, side_rows, w_gate.shape[2]), lambda i, j, kk: (l, step(i, j, kk), 0)),
            pl.BlockSpec((side_rows, 1), lambda i, j, kk: (step(i, j, kk), 0)),
        ],
        out_specs=[tile, tile, pl.BlockSpec((tm, LANES), lambda i, j, kk: (i, j)),
                   pl.BlockSpec((side_rows, w_gate.shape[2]), lambda i, j, kk: (step(i, j, kk), 0))],
        out_shape=[jax.ShapeDtypeStruct((m, n), F32), jax.ShapeDtypeStruct((m, n), BF16),
                   jax.ShapeDtypeStruct((m, n_j * LANES), F32),
                   jax.ShapeDtypeStruct(w_gate.shape[1:], BF16)],
        compiler_params=_params(("arbitrary", "arbitrary", "arbitrary")),
        name="mlp_down",
    )(a, w, resid, w_gate, g_gate)


def _row_scale_kernel(ssq_ref, r_ref):
    s = ssq_ref[:, 0:LANES]
    for j in range(1, ssq_ref.shape[1] // LANES):
        s = s + ssq_ref[:, j * LANES:(j + 1) * LANES]
    r_ref[...] = lax.rsqrt(s * (1.0 / D_MODEL) + EPS)


def _row_scale(ssq, name, tm=1024):
    m, w = ssq.shape
    return pl.pallas_call(
        _row_scale_kernel,
        grid=(m // tm,),
        in_specs=[pl.BlockSpec((tm, w), lambda i: (i, 0))],
        out_specs=pl.BlockSpec((tm, LANES), lambda i: (i, 0)),
        out_shape=jax.ShapeDtypeStruct((m, LANES), F32),
        compiler_params=_params(("parallel",)),
        name=name,
    )(ssq)


def _chunk_scan(x, pos, combine, fill):
    sh = 1
    while sh < CHUNK:
        prev = jnp.where(pos >= sh, pltpu.roll(x, sh, 0), fill)
        x = combine(x, prev)
        sh *= 2
    return x


def _gates_kernel(a_ref, w_ref, bias_ref, li_ref, b_ref, mx_ref, lirow_ref, brow_ref):
    tm = a_ref.shape[0]
    pre = jnp.dot(a_ref[...], w_ref[...], preferred_element_type=F32) + bias_ref[...]
    pre = GATE_CAP * jnp.tanh(pre / GATE_CAP)
    logf = jnp.minimum(pre, 0.0) - jnp.log1p(jnp.exp(-jnp.abs(pre)))
    pos = lax.broadcasted_iota(jnp.int32, (tm, LANES), 0) % CHUNK
    b = _chunk_scan(logf, pos, jnp.add, 0.0)
    b = pltpu.roll(b, LANES - N_HEADS, 1)
    li = pre
    mx = b + _chunk_scan(li - b, pos, jnp.maximum, -jnp.inf)
    li_ref[...] = li
    b_ref[...] = b
    mx_ref[...] = mx
    for c in range(tm // CHUNK):
        rows = slice(c * CHUNK, (c + 1) * CHUNK)
        lirow_ref[c] = li[rows].T[:N_HEADS]
        brow_ref[c] = b[rows].T[:N_HEADS]


def _gates(hn, w_g, bias, tm=1024):
    m, k = hn.shape
    col = jax.ShapeDtypeStruct((m, LANES), F32)
    row = jax.ShapeDtypeStruct((m // CHUNK, N_HEADS, CHUNK), F32)
    col_spec = pl.BlockSpec((tm, LANES), lambda i: (i, 0))
    row_spec = pl.BlockSpec((tm // CHUNK, N_HEADS, CHUNK), lambda i: (i, 0, 0))
    return pl.pallas_call(
        _gates_kernel,
        grid=(m // tm,),
        in_specs=[
            pl.BlockSpec((tm, k), lambda i: (i, 0)),
            pl.BlockSpec((k, LANES), lambda i: (0, 0)),
            pl.BlockSpec((1, LANES), lambda i: (0, 0)),
        ],
        out_specs=[col_spec, col_spec, col_spec, row_spec, row_spec],
        out_shape=[col, col, col, row, row],
        compiler_params=_params(("parallel",)),
        name="gates",
    )(hn, w_g, bias)


def _pool_kernel(u_ref, w_ref, scale_ref, o_ref, ext_ref):
    ts = u_ref.shape[0]
    i = pl.program_id(0)

    @pl.when(i == 0)
    def _():
        ext_ref[0:POOL_HALO, :] = jnp.zeros((POOL_HALO, D_POOL), F32)

    @pl.when(i != 0)
    def _():
        ext_ref[0:POOL_HALO, :] = ext_ref[ts:ts + POOL_HALO, :]

    ext_ref[POOL_HALO:POOL_HALO + ts, :] = u_ref[...]
    t = i * ts + lax.broadcasted_iota(jnp.int32, (ts, 1), 0)
    for g, win in enumerate(POOL_WINDOWS):
        cols = slice(g * POOL_GROUP, (g + 1) * POOL_GROUP)
        u = ext_ref[POOL_HALO:POOL_HALO + ts, cols]
        win_sum = u
        for j in range(1, win):
            win_sum = win_sum + ext_ref[POOL_HALO - j:POOL_HALO - j + ts, cols]
        cnt = jnp.minimum(t + 1, win).astype(F32)
        z = (win_sum / cnt - u).astype(BF16)
        y = jnp.dot(z, w_ref[g], preferred_element_type=F32)
        o_ref[:, cols] = (y * scale_ref[:, cols]).astype(o_ref.dtype)


def _pool(proj, w_pool, scale, ts=512):
    s = proj.shape[0]
    return pl.pallas_call(
        _pool_kernel,
        grid=(s // ts,),
        in_specs=[
            pl.BlockSpec((ts, D_POOL), lambda i: (i, 0)),
            pl.BlockSpec((len(POOL_WINDOWS), POOL_GROUP, POOL_GROUP), lambda i: (0, 0, 0)),
            pl.BlockSpec((1, D_POOL), lambda i: (0, 0)),
        ],
        out_specs=pl.BlockSpec((ts, D_POOL), lambda i: (i, 0)),
        out_shape=jax.ShapeDtypeStruct((s, D_POOL), BF16),
        scratch_shapes=[pltpu.VMEM((ts + POOL_HALO, D_POOL), F32)],
        compiler_params=_params(("arbitrary",)),
        name="pool_mixer",
    )(proj, w_pool, scale)


def _mlstm_kernel(qk_ref, v_ref, o_ref, li_ref, b_ref, mx_ref, lirow_ref, brow_ref,
                  cw_ref, cb_ref, gh_ref, y_ref, ext_ref, c_ref, n_ref, m_ref):
    L = CHUNK
    i = pl.program_id(0)

    @pl.when(i == 0)
    def _():
        ext_ref[0:CONV_HALO, :] = jnp.zeros((CONV_HALO, D_QK), F32)
        c_ref[...] = jnp.zeros_like(c_ref)
        n_ref[...] = jnp.zeros_like(n_ref)
        m_ref[...] = jnp.zeros_like(m_ref)

    @pl.when(i != 0)
    def _():
        ext_ref[0:CONV_HALO, :] = ext_ref[L:L + CONV_HALO, :]

    ext_ref[CONV_HALO:CONV_HALO + L, :] = qk_ref[...]

    t_idx = lax.broadcasted_iota(jnp.int32, (L, L), 0)
    s_idx = lax.broadcasted_iota(jnp.int32, (L, L), 1)
    causal = s_idx <= t_idx

    def conv_silu(cols):
        y = cb_ref[:, cols]
        for j in range(CONV_W):
            off = CONV_HALO - (CONV_W - 1) + j
            y = y + cw_ref[j:j + 1, cols] * ext_ref[off:off + L, cols]
        return y * jax.nn.sigmoid(y)

    for h in range(N_HEADS):
        q = conv_silu(slice(h * DQK, (h + 1) * DQK))
        k = conv_silu(slice(N_HEADS * DQK + h * DQK, N_HEADS * DQK + (h + 1) * DQK)) * (DQK ** -0.5)
        vcols = slice(h * DV, (h + 1) * DV)
        v = v_ref[:, vcols].astype(BF16)
        li_col = li_ref[:, h:h + 1]
        b_col = b_ref[:, h:h + 1]
        mx_col = mx_ref[:, h:h + 1]
        li_row = lirow_ref[0, h:h + 1, :]
        b_row = brow_ref[0, h:h + 1, :]
        m_prev = m_ref[h][:, 0:1]
        c_prev = c_ref[h]
        n_prev = n_ref[h]

        qb = q.astype(BF16)
        kb = k.astype(BF16)
        d_mat = jnp.where(causal, b_col - b_row + li_row, -jnp.inf)
        e_col = b_col + m_prev
        m_t = jnp.maximum(mx_col, e_col)
        s_qk = lax.dot_general(qb, kb, (((1,), (1,)), ((), ())), preferred_element_type=F32)
        w_mat = jnp.exp(d_mat - m_t) * s_qk
        w_inter = jnp.exp(e_col - m_t)
        num = jnp.dot(w_mat.astype(BF16), v, preferred_element_type=F32)
        num = num + w_inter * jnp.dot(qb, c_prev.astype(BF16), preferred_element_type=F32)
        den = jnp.sum(w_mat, axis=-1, keepdims=True) + w_inter * jnp.sum(q * n_prev, axis=-1, keepdims=True)
        hh = num / jnp.maximum(jnp.abs(den), jnp.exp(-m_t))
        hh = hh * lax.rsqrt(jnp.mean(hh * hh, axis=-1, keepdims=True) + EPS)
        hh = hh * gh_ref[:, vcols] * jax.nn.sigmoid(o_ref[:, vcols])
        y_ref[:, vcols] = hh.astype(y_ref.dtype)

        g_tot = b_col[L - 1:L, :]
        a_col = g_tot + li_col - b_col
        m_loc = jnp.max(a_col, axis=0, keepdims=True)
        kw = k * jnp.exp(a_col - m_loc)
        c_loc = lax.dot_general(kw.astype(BF16), v, (((0,), (0,)), ((), ())), preferred_element_type=F32)
        n_loc = jnp.sum(kw, axis=0, keepdims=True)
        m_new = jnp.maximum(g_tot + m_prev, m_loc)
        s_old = jnp.exp(g_tot + m_prev - m_new)
        s_loc = jnp.exp(m_loc - m_new)
        c_ref[h] = s_old * c_prev + s_loc * c_loc
        n_ref[h] = s_old * n_prev + s_loc * n_loc
        m_ref[h] = jnp.broadcast_to(m_new, (1, LANES))


def _mlstm(proj, li, b, mx, lirow, brow, conv_w, conv_b, g_head):
    s = proj.shape[0]
    L = CHUNK
    col_spec = pl.BlockSpec((L, LANES), lambda i: (i, 0))
    row_spec = pl.BlockSpec((1, N_HEADS, L), lambda i: (i, 0, 0))
    return pl.pallas_call(
        _mlstm_kernel,
        grid=(s // L,),
        in_specs=[
            pl.BlockSpec((L, D_QK), lambda i: (i, D_POOL // D_QK)),
            pl.BlockSpec((L, D_MLSTM), lambda i: (i, (D_POOL + D_QK) // D_MLSTM)),
            pl.BlockSpec((L, D_MLSTM), lambda i: (i, (D_POOL + D_QK + D_MLSTM) // D_MLSTM)),
            col_spec, col_spec, col_spec, row_spec, row_spec,
            pl.BlockSpec((CONV_W, D_QK), lambda i: (0, 0)),
            pl.BlockSpec((1, D_QK), lambda i: (0, 0)),
            pl.BlockSpec((1, D_MLSTM), lambda i: (0, 0)),
        ],
        out_specs=pl.BlockSpec((L, D_MLSTM), lambda i: (i, 0)),
        out_shape=jax.ShapeDtypeStruct((s, D_MLSTM), BF16),
        scratch_shapes=[
            pltpu.VMEM((L + CONV_HALO, D_QK), F32),
            pltpu.VMEM((N_HEADS, DQK, DV), F32),
            pltpu.VMEM((N_HEADS, 1, DQK), F32),
            pltpu.VMEM((N_HEADS, 1, LANES), F32),
        ],
        compiler_params=_params(("arbitrary",)),
        name="mlstm_mixer",
    )(proj, proj, proj, li, b, mx, lirow, brow, conv_w, conv_b, g_head)


def kernel(x, p, g_mix, w_in, conv_w, conv_b, b_igate, b_fgate, g_head, w_pool, pool_scale,
           w_out, g_mlp, w_up, w_down, g_ple, w_ple_gate, b_ple_gate, w_ple, g_final):
    depth = w_in.shape[0]
    h = x.reshape(SEQ, D_MODEL)
    for l in range(depth):
        w_g = jnp.pad(w_in[l, :, D_MAIN:], ((0, 0), (0, LANES - 2 * N_HEADS))).astype(BF16)
        gate_bias = jnp.pad(jnp.concatenate([b_igate[l], b_fgate[l]]), (0, LANES - 2 * N_HEADS)).reshape(1, LANES)

        hn = _rmsnorm(h, g_mix[l], BF16, "norm_mix")
        proj = _proj_in(l, hn, jnp.swapaxes(w_in, 1, 2))
        li, b, mx, lirow, brow = _gates(hn, w_g, gate_bias.astype(F32))
        y_pool = _pool(proj, w_pool[l].astype(BF16), pool_scale[l].reshape(1, D_POOL))
        y_mlstm = _mlstm(proj, li, b, mx, lirow, brow, conv_w[l], conv_b[l].reshape(1, D_QK),
                         g_head[l].reshape(1, D_MLSTM))
        h, hb, ssq = _mix_out(l, y_pool, y_mlstm, w_out, h)

        r = _row_scale(ssq, "scale_mlp")
        act, w_down_b = _mlp_up(l, hb, w_up, g_mlp[l].reshape(D_MODEL, 1), r, w_down)
        h, hb, ssq, w_gate_b = _down(l, act, w_down_b, h, w_ple_gate, g_ple[l].reshape(D_MODEL, 1))

        assert depth == 1
        h = _ple_final(hb, w_gate_b, ssq, b_ple_gate[l].reshape(1, D_MODEL), p[l].reshape(SEQ, D_PLE).astype(BF16),
                       w_ple[l].astype(BF16), h, g_final.reshape(1, D_MODEL))
    return h.reshape(x.shape)
```

```python
import jax
import jax.numpy as jnp
from jax import lax
from jax.experimental import pallas as pl
from jax.experimental.pallas import tpu as pltpu

F32 = jnp.float32
BF16 = jnp.bfloat16

D_MODEL = 4096
SEQ = 8192
D_POOL = 2048
POOL_WINDOWS = (2, 4, 8, 16)
POOL_GROUP = 512
POOL_HALO = 16
D_MLSTM = 2048
N_HEADS = 8
DV = 256
DQK = 128
D_QK = 2 * N_HEADS * DQK
CONV_W = 4
CONV_HALO = 8
GATE_CAP = 15.0
D_FF = 4 * D_MODEL
D_PLE = 256
EPS = 1e-6
D_MAIN = D_POOL + D_QK + 2 * D_MLSTM
LANES = 128
CHUNK = 128
VMEM_LIMIT = 58 * 1024 * 1024


def _params(sem, vmem=VMEM_LIMIT):
    return pltpu.CompilerParams(dimension_semantics=sem, vmem_limit_bytes=vmem)


def _rmsnorm_kernel(x_ref, g_ref, o_ref):
    x = x_ref[...]
    ms = jnp.mean(x * x, axis=-1, keepdims=True)
    o_ref[...] = (x * lax.rsqrt(ms + EPS) * g_ref[...]).astype(o_ref.dtype)


def _rmsnorm(x, g, out_dtype, name, tm=256):
    s, d = x.shape
    return pl.pallas_call(
        _rmsnorm_kernel,
        grid=(s // tm,),
        in_specs=[pl.BlockSpec((tm, d), lambda i: (i, 0)), pl.BlockSpec((1, d), lambda i: (0, 0))],
        out_specs=pl.BlockSpec((tm, d), lambda i: (i, 0)),
        out_shape=jax.ShapeDtypeStruct((s, d), out_dtype),
        compiler_params=_params(("parallel",)),
        name=name,
    )(x, g.reshape(1, d).astype(F32))


def _ws_maps(nb, n_i):
    row = lambda b, i: jnp.where(b > 0, i, 0)
    blk = lambda b: jnp.maximum(b - 1, 0)
    wrow = lambda b, i: jnp.where(b < nb, i, n_i - 1)
    wblk = lambda b: jnp.minimum(b, nb - 1)
    return row, blk, wrow, wblk


def _ws_step(wchunk_ref, g_ref, wbufs, compute, transposed=False):
    b = pl.program_id(0)
    i = pl.program_id(1)

    def stage(buf):
        w = wchunk_ref[...]
        if transposed:
            w = w.T
        if g_ref is not None:
            w = w * g_ref[...]
        ck = w.shape[0]
        buf[pl.ds(pl.multiple_of(i * ck, ck), ck), :] = w.astype(BF16)

    @pl.when(b == 0)
    def _():
        stage(wbufs[0])

    for parity in (0, 1):
        @pl.when(jnp.logical_and(b > 0, b % 2 == parity))
        def _():
            stage(wbufs[parity])
            compute(wbufs[1 - parity])


def _row_sumsq(h):
    return jnp.broadcast_to(jnp.sum(h * h, axis=1, keepdims=True), (h.shape[0], LANES))


def _lane_tile(r, n):
    return jnp.concatenate([r] * (n // LANES), axis=1)


def _proj_kernel(a_ref, wchunk_ref, o_ref, wbuf0, wbuf1):
    def compute(w_ref):
        o_ref[...] = jnp.dot(a_ref[...], w_ref[...], preferred_element_type=F32)
    _ws_step(wchunk_ref, None, (wbuf0, wbuf1), compute, transposed=True)


def _mix_out_kernel(yp_ref, ym_ref, wchunk_ref, res_ref, o_ref, ob_ref, ssq_ref, wbuf0, wbuf1):
    def compute(w_ref):
        acc = jnp.dot(yp_ref[...], w_ref[0:D_POOL, :], preferred_element_type=F32)
        acc += jnp.dot(ym_ref[...], w_ref[D_POOL:D_POOL + D_MLSTM, :], preferred_element_type=F32)
        h = res_ref[...] + acc
        o_ref[...] = h
        ob_ref[...] = h.astype(BF16)
        ssq_ref[...] = _row_sumsq(h)
    _ws_step(wchunk_ref, None, (wbuf0, wbuf1), compute)


def _up_kernel(a_ref, wchunk_ref, g_ref, r_ref, wd_ref, o_ref, wdb_ref, wbuf0, wbuf1):
    def compute(w_ref):
        y = jnp.dot(a_ref[...], w_ref[...], preferred_element_type=F32)
        y = jnp.maximum(y * _lane_tile(r_ref[...], y.shape[1]), 0.0)
        o_ref[...] = (y * y).astype(BF16)
        wdb_ref[...] = wd_ref[...].astype(BF16)
    _ws_step(wchunk_ref, g_ref, (wbuf0, wbuf1), compute)


def _ws_call(kern, name, l, a_list, w, k_dim, n_out, tm, tn, extra_in, extra_specs, out_shapes, out_specs,
             transposed=False):
    m = a_list[0].shape[0]
    nb, n_i = n_out // tn, m // tm
    ck = k_dim // n_i
    row, blk, wrow, wblk = _ws_maps(nb, n_i)
    a_specs = [pl.BlockSpec((tm, a.shape[1]), lambda b, i: (row(b, i), 0)) for a in a_list]
    if transposed:
        w_spec = pl.BlockSpec((None, tn, ck), lambda b, i: (l, wblk(b), wrow(b, i)))
    else:
        w_spec = pl.BlockSpec((None, ck, tn), lambda b, i: (l, wrow(b, i), wblk(b)))
    return pl.pallas_call(
        kern,
        grid=(nb + 1, n_i),
        in_specs=a_specs + [w_spec] + extra_specs(row, blk, wrow, wblk),
        out_specs=out_specs(row, blk),
        out_shape=out_shapes,
        scratch_shapes=[pltpu.VMEM((k_dim, tn), BF16), pltpu.VMEM((k_dim, tn), BF16)],
        compiler_params=_params(("arbitrary", "arbitrary")),
        name=name,
    )(*a_list, w, *extra_in)


def _proj_in(l, hn, w_in_t, tm=1024, tn=1024):
    m = hn.shape[0]
    return _ws_call(
        _proj_kernel, "proj_in", l, [hn], w_in_t, D_MODEL, D_MAIN, tm, tn, [],
        lambda row, blk, wrow, wblk: [],
        jax.ShapeDtypeStruct((m, D_MAIN), F32),
        lambda row, blk: pl.BlockSpec((tm, tn), lambda b, i: (row(b, i), blk(b))),
        transposed=True)


def _mix_out(l, y_pool, y_mlstm, w_out, resid, tm=512, tn=1024):
    m = resid.shape[0]
    nb = D_MODEL // tn
    tile = lambda row, blk: pl.BlockSpec((tm, tn), lambda b, i: (row(b, i), blk(b)))
    return _ws_call(
        _mix_out_kernel, "mix_out", l, [y_pool, y_mlstm], w_out, D_MODEL, D_MODEL, tm, tn, [resid],
        lambda row, blk, wrow, wblk: [tile(row, blk)],
        [jax.ShapeDtypeStruct((m, D_MODEL), F32), jax.ShapeDtypeStruct((m, D_MODEL), BF16),
         jax.ShapeDtypeStruct((m, nb * LANES), F32)],
        lambda row, blk: [tile(row, blk), tile(row, blk),
                          pl.BlockSpec((tm, LANES), lambda b, i: (row(b, i), blk(b)))])


def _mlp_up(l, hb, w_up, g, r, w_down, tm=1024, tn=1024):
    m = hb.shape[0]
    nb, n_i = D_FF // tn, m // tm
    ck = D_MODEL // n_i
    wd_rows = D_FF // (nb * n_i)
    side = lambda row, blk: (lambda b, i: (blk(b) * n_i + row(b, i), 0))
    return _ws_call(
        _up_kernel, "mlp_up", l, [hb], w_up, D_MODEL, D_FF, tm, tn, [g, r, w_down],
        lambda row, blk, wrow, wblk: [
            pl.BlockSpec((ck, 1), lambda b, i: (wrow(b, i), 0)),
            pl.BlockSpec((tm, LANES), lambda b, i: (row(b, i), 0)),
            pl.BlockSpec((None, wd_rows, D_MODEL), lambda b, i: (l,) + side(row, blk)(b, i)),
        ],
        [jax.ShapeDtypeStruct((m, D_FF), BF16), jax.ShapeDtypeStruct((D_FF, D_MODEL), BF16)],
        lambda row, blk: [pl.BlockSpec((tm, tn), lambda b, i: (row(b, i), blk(b))),
                          pl.BlockSpec((wd_rows, D_MODEL), side(row, blk))])


def _ple_kernel(a_ref, w_ref, ssq_ref, bias_ref, p_ref, wp_ref, res_ref, gf_ref, o_ref):
    j = pl.program_id(1)
    tn = w_ref.shape[1]
    n_j = o_ref.shape[1] // tn

    def tile():
        s = ssq_ref[:, 0:LANES]
        for c in range(1, ssq_ref.shape[1] // LANES):
            s = s + ssq_ref[:, c * LANES:(c + 1) * LANES]
        r = lax.rsqrt(s * (1.0 / D_MODEL) + EPS)
        z = jnp.dot(a_ref[...], w_ref[...], preferred_element_type=F32)
        z = z * _lane_tile(r, tn) + bias_ref[...]
        pe = jnp.dot(p_ref[...], wp_ref[...], preferred_element_type=F32)
        return res_ref[...] + jax.nn.sigmoid(z) * pe

    for jj in range(n_j):
        @pl.when(j == jj)
        def _():
            o_ref[:, jj * tn:(jj + 1) * tn] = tile()
            if jj == n_j - 1:
                x = o_ref[...]
                ms = jnp.mean(x * x, axis=-1, keepdims=True)
                o_ref[...] = x * lax.rsqrt(ms + EPS) * gf_ref[...]


def _ple_final(hb, w_gate_b, ssq, b_gate, p, w_ple, resid, g_final, tm=512, tn=1024):
    m = hb.shape[0]
    return pl.pallas_call(
        _ple_kernel,
        grid=(m // tm, D_MODEL // tn),
        in_specs=[
            pl.BlockSpec((tm, D_MODEL), lambda i, j: (i, 0)),
            pl.BlockSpec((D_MODEL, tn), lambda i, j: (0, j)),
            pl.BlockSpec((tm, ssq.shape[1]), lambda i, j: (i, 0)),
            pl.BlockSpec((1, tn), lambda i, j: (0, j)),
            pl.BlockSpec((tm, D_PLE), lambda i, j: (i, 0)),
            pl.BlockSpec((D_PLE, tn), lambda i, j: (0, j)),
            pl.BlockSpec((tm, tn), lambda i, j: (i, j)),
            pl.BlockSpec((1, D_MODEL), lambda i, j: (0, 0)),
        ],
        out_specs=pl.BlockSpec((tm, D_MODEL), lambda i, j: (i, 0)),
        out_shape=jax.ShapeDtypeStruct((m, D_MODEL), F32),
        compiler_params=_params(("parallel", "arbitrary")),
        name="ple_final",
    )(hb, w_gate_b, ssq, b_gate, p, w_ple, resid, g_final)


def _down_kernel(a_ref, w_ref, res_hbm, wg_ref, g_ref, o_ref, ob_ref, ssq_ref, wgb_ref, res_sem):
    i, j, k = pl.program_id(0), pl.program_id(1), pl.program_id(2)
    last = pl.num_programs(2) - 1
    tm, tn = o_ref.shape
    wgb_ref[...] = (wg_ref[...] * g_ref[...]).astype(BF16)

    def part():
        return jnp.dot(a_ref[...], w_ref[...], preferred_element_type=F32)

    @pl.when(k == 0)
    def _():
        rows = pl.ds(pl.multiple_of(i * tm, tm), tm)
        cols = pl.ds(pl.multiple_of(j * tn, tn), tn)
        res_copy = pltpu.make_async_copy(res_hbm.at[rows, cols], o_ref, res_sem)
        res_copy.start()
        p = part()
        res_copy.wait()
        o_ref[...] += p

    @pl.when(jnp.logical_and(k != 0, k != last))
    def _():
        o_ref[...] += part()

    @pl.when(k == last)
    def _():
        h = o_ref[...] + part()
        o_ref[...] = h
        ob_ref[...] = h.astype(BF16)
        ssq_ref[...] = _row_sumsq(h)


def _down(l, a, w, resid, w_gate, g_gate, tm=1024, tn=1024, tk=4096):
    m, k = a.shape
    n = w.shape[1]
    n_j, n_k = n // tn, k // tk
    side_rows = w_gate.shape[1] // ((m // tm) * n_j * n_k)
    tile = pl.BlockSpec((tm, tn), lambda i, j, kk: (i, j))
    step = lambda i, j, kk: (i * n_j + j) * n_k + kk
    return pl.pallas_call(
        _down_kernel,
        grid=(m // tm, n_j, n_k),
        in_specs=[
            pl.BlockSpec((tm, tk), lambda i, j, kk: (i, kk)),
            pl.BlockSpec((tk, tn), lambda i, j, kk: (kk, j)),
            pl.BlockSpec(memory_space=pl.ANY),
            pl.BlockSpec((None, side_rows, w_gate.shape[2]), lambda i, j, kk: (l, step(i, j, kk), 0)),
            pl.BlockSpec((side_rows, 1), lambda i, j, kk: (step(i, j, kk), 0)),
        ],
        out_specs=[tile, tile, pl.BlockSpec((tm, LANES), lambda i, j, kk: (i, j)),
                   pl.BlockSpec((side_rows, w_gate.shape[2]), lambda i, j, kk: (step(i, j, kk), 0))],
        out_shape=[jax.ShapeDtypeStruct((m, n), F32), jax.ShapeDtypeStruct((m, n), BF16),
                   jax.ShapeDtypeStruct((m, n_j * LANES), F32),
                   jax.ShapeDtypeStruct(w_gate.shape[1:], BF16)],
        scratch_shapes=[pltpu.SemaphoreType.DMA(())],
        compiler_params=_params(("arbitrary", "arbitrary", "arbitrary")),
        name="mlp_down",
    )(a, w, resid, w_gate, g_gate)


def _row_scale_kernel(ssq_ref, r_ref):
    s = ssq_ref[:, 0:LANES]
    for j in range(1, ssq_ref.shape[1] // LANES):
        s = s + ssq_ref[:, j * LANES:(j + 1) * LANES]
    r_ref[...] = lax.rsqrt(s * (1.0 / D_MODEL) + EPS)


def _row_scale(ssq, name, tm=1024):
    m, w = ssq.shape
    return pl.pallas_call(
        _row_scale_kernel,
        grid=(m // tm,),
        in_specs=[pl.BlockSpec((tm, w), lambda i: (i, 0))],
        out_specs=pl.BlockSpec((tm, LANES), lambda i: (i, 0)),
        out_shape=jax.ShapeDtypeStruct((m, LANES), F32),
        compiler_params=_params(("parallel",)),
        name=name,
    )(ssq)


def _chunk_scan(x, pos, combine, fill):
    sh = 1
    while sh < CHUNK:
        prev = jnp.where(pos >= sh, pltpu.roll(x, sh, 0), fill)
        x = combine(x, prev)
        sh *= 2
    return x


def _gates_kernel(a_ref, w_ref, bias_ref, li_ref, b_ref, mx_ref, lirow_ref, brow_ref):
    tm = a_ref.shape[0]
    pre = jnp.dot(a_ref[...], w_ref[...], preferred_element_type=F32) + bias_ref[...]
    pre = GATE_CAP * jnp.tanh(pre / GATE_CAP)
    logf = jnp.minimum(pre, 0.0) - jnp.log1p(jnp.exp(-jnp.abs(pre)))
    pos = lax.broadcasted_iota(jnp.int32, (tm, LANES), 0) % CHUNK
    b = _chunk_scan(logf, pos, jnp.add, 0.0)
    b = pltpu.roll(b, LANES - N_HEADS, 1)
    li = pre
    mx = b + _chunk_scan(li - b, pos, jnp.maximum, -jnp.inf)
    li_ref[...] = li
    b_ref[...] = b
    mx_ref[...] = mx
    for c in range(tm // CHUNK):
        rows = slice(c * CHUNK, (c + 1) * CHUNK)
        lirow_ref[c] = li[rows].T[:N_HEADS]
        brow_ref[c] = b[rows].T[:N_HEADS]


def _gates(hn, w_g, bias, tm=1024):
    m, k = hn.shape
    col = jax.ShapeDtypeStruct((m, LANES), F32)
    row = jax.ShapeDtypeStruct((m // CHUNK, N_HEADS, CHUNK), F32)
    col_spec = pl.BlockSpec((tm, LANES), lambda i: (i, 0))
    row_spec = pl.BlockSpec((tm // CHUNK, N_HEADS, CHUNK), lambda i: (i, 0, 0))
    return pl.pallas_call(
        _gates_kernel,
        grid=(m // tm,),
        in_specs=[
            pl.BlockSpec((tm, k), lambda i: (i, 0)),
            pl.BlockSpec((k, LANES), lambda i: (0, 0)),
            pl.BlockSpec((1, LANES), lambda i: (0, 0)),
        ],
        out_specs=[col_spec, col_spec, col_spec, row_spec, row_spec],
        out_shape=[col, col, col, row, row],
        compiler_params=_params(("parallel",)),
        name="gates",
    )(hn, w_g, bias)


def _pool_kernel(u_ref, w_ref, scale_ref, o_ref, ext_ref):
    ts = u_ref.shape[0]
    i = pl.program_id(0)

    @pl.when(i == 0)
    def _():
        ext_ref[0:POOL_HALO, :] = jnp.zeros((POOL_HALO, D_POOL), F32)

    @pl.when(i != 0)
    def _():
        ext_ref[0:POOL_HALO, :] = ext_ref[ts:ts + POOL_HALO, :]

    ext_ref[POOL_HALO:POOL_HALO + ts, :] = u_ref[...]
    t = i * ts + lax.broadcasted_iota(jnp.int32, (ts, 1), 0)
    for g, win in enumerate(POOL_WINDOWS):
        cols = slice(g * POOL_GROUP, (g + 1) * POOL_GROUP)
        u = ext_ref[POOL_HALO:POOL_HALO + ts, cols]
        win_sum = u
        for j in range(1, win):
            win_sum = win_sum + ext_ref[POOL_HALO - j:POOL_HALO - j + ts, cols]
        cnt = jnp.minimum(t + 1, win).astype(F32)
        z = (win_sum / cnt - u).astype(BF16)
        y = jnp.dot(z, w_ref[g], preferred_element_type=F32)
        o_ref[:, cols] = (y * scale_ref[:, cols]).astype(o_ref.dtype)


def _pool(proj, w_pool, scale, ts=512):
    s = proj.shape[0]
    return pl.pallas_call(
        _pool_kernel,
        grid=(s // ts,),
        in_specs=[
            pl.BlockSpec((ts, D_POOL), lambda i: (i, 0)),
            pl.BlockSpec((len(POOL_WINDOWS), POOL_GROUP, POOL_GROUP), lambda i: (0, 0, 0)),
            pl.BlockSpec((1, D_POOL), lambda i: (0, 0)),
        ],
        out_specs=pl.BlockSpec((ts, D_POOL), lambda i: (i, 0)),
        out_shape=jax.ShapeDtypeStruct((s, D_POOL), BF16),
        scratch_shapes=[pltpu.VMEM((ts + POOL_HALO, D_POOL), F32)],
        compiler_params=_params(("arbitrary",)),
        name="pool_mixer",
    )(proj, w_pool, scale)


def _mlstm_kernel(qk_ref, v_ref, o_ref, li_ref, b_ref, mx_ref, lirow_ref, brow_ref,
                  cw_ref, cb_ref, gh_ref, y_ref, ext_ref, c_ref, n_ref, m_ref):
    L = CHUNK
    i = pl.program_id(0)

    @pl.when(i == 0)
    def _():
        ext_ref[0:CONV_HALO, :] = jnp.zeros((CONV_HALO, D_QK), F32)
        c_ref[...] = jnp.zeros_like(c_ref)
        n_ref[...] = jnp.zeros_like(n_ref)
        m_ref[...] = jnp.zeros_like(m_ref)

    @pl.when(i != 0)
    def _():
        ext_ref[0:CONV_HALO, :] = ext_ref[L:L + CONV_HALO, :]

    ext_ref[CONV_HALO:CONV_HALO + L, :] = qk_ref[...]

    t_idx = lax.broadcasted_iota(jnp.int32, (L, L), 0)
    s_idx = lax.broadcasted_iota(jnp.int32, (L, L), 1)
    causal = s_idx <= t_idx

    def conv_silu(cols):
        y = cb_ref[:, cols]
        for j in range(CONV_W):
            off = CONV_HALO - (CONV_W - 1) + j
            y = y + cw_ref[j:j + 1, cols] * ext_ref[off:off + L, cols]
        return y * jax.nn.sigmoid(y)

    for h in range(N_HEADS):
        q = conv_silu(slice(h * DQK, (h + 1) * DQK))
        k = conv_silu(slice(N_HEADS * DQK + h * DQK, N_HEADS * DQK + (h + 1) * DQK)) * (DQK ** -0.5)
        vcols = slice(h * DV, (h + 1) * DV)
        v = v_ref[:, vcols].astype(BF16)
        li_col = li_ref[:, h:h + 1]
        b_col = b_ref[:, h:h + 1]
        mx_col = mx_ref[:, h:h + 1]
        li_row = lirow_ref[0, h:h + 1, :]
        b_row = brow_ref[0, h:h + 1, :]
        m_prev = m_ref[h][:, 0:1]
        c_prev = c_ref[h]
        n_prev = n_ref[h]

        qb = q.astype(BF16)
        kb = k.astype(BF16)
        d_mat = jnp.where(causal, b_col - b_row + li_row, -jnp.inf)
        e_col = b_col + m_prev
        m_t = jnp.maximum(mx_col, e_col)
        s_qk = lax.dot_general(qb, kb, (((1,), (1,)), ((), ())), preferred_element_type=F32)
        w_mat = jnp.exp(d_mat - m_t) * s_qk
        w_inter = jnp.exp(e_col - m_t)
        num = jnp.dot(w_mat.astype(BF16), v, preferred_element_type=F32)
        num = num + w_inter * jnp.dot(qb, c_prev.astype(BF16), preferred_element_type=F32)
        den = jnp.sum(w_mat, axis=-1, keepdims=True) + w_inter * jnp.sum(q * n_prev, axis=-1, keepdims=True)
        hh = num / jnp.maximum(jnp.abs(den), jnp.exp(-m_t))
        hh = hh * lax.rsqrt(jnp.mean(hh * hh, axis=-1, keepdims=True) + EPS)
        hh = hh * gh_ref[:, vcols] * jax.nn.sigmoid(o_ref[:, vcols])
        y_ref[:, vcols] = hh.astype(y_ref.dtype)

        g_tot = b_col[L - 1:L, :]
        a_col = g_tot + li_col - b_col
        m_loc = jnp.max(a_col, axis=0, keepdims=True)
        kw = k * jnp.exp(a_col - m_loc)
        c_loc = lax.dot_general(kw.astype(BF16), v, (((0,), (0,)), ((), ())), preferred_element_type=F32)
        n_loc = jnp.sum(kw, axis=0, keepdims=True)
        m_new = jnp.maximum(g_tot + m_prev, m_loc)
        s_old = jnp.exp(g_tot + m_prev - m_new)
        s_loc = jnp.exp(m_loc - m_new)
        c_ref[h] = s_old * c_prev + s_loc * c_loc
        n_ref[h] = s_old * n_prev + s_loc * n_loc
        m_ref[h] = jnp.broadcast_to(m_new, (1, LANES))


def _mlstm(proj, li, b, mx, lirow, brow, conv_w, conv_b, g_head):
    s = proj.shape[0]
    L = CHUNK
    col_spec = pl.BlockSpec((L, LANES), lambda i: (i, 0))
    row_spec = pl.BlockSpec((1, N_HEADS, L), lambda i: (i, 0, 0))
    return pl.pallas_call(
        _mlstm_kernel,
        grid=(s // L,),
        in_specs=[
            pl.BlockSpec((L, D_QK), lambda i: (i, D_POOL // D_QK)),
            pl.BlockSpec((L, D_MLSTM), lambda i: (i, (D_POOL + D_QK) // D_MLSTM)),
            pl.BlockSpec((L, D_MLSTM), lambda i: (i, (D_POOL + D_QK + D_MLSTM) // D_MLSTM)),
            col_spec, col_spec, col_spec, row_spec, row_spec,
            pl.BlockSpec((CONV_W, D_QK), lambda i: (0, 0)),
            pl.BlockSpec((1, D_QK), lambda i: (0, 0)),
            pl.BlockSpec((1, D_MLSTM), lambda i: (0, 0)),
        ],
        out_specs=pl.BlockSpec((L, D_MLSTM), lambda i: (i, 0)),
        out_shape=jax.ShapeDtypeStruct((s, D_MLSTM), BF16),
        scratch_shapes=[
            pltpu.VMEM((L + CONV_HALO, D_QK), F32),
            pltpu.VMEM((N_HEADS, DQK, DV), F32),
            pltpu.VMEM((N_HEADS, 1, DQK), F32),
            pltpu.VMEM((N_HEADS, 1, LANES), F32),
        ],
        compiler_params=_params(("arbitrary",)),
        name="mlstm_mixer",
    )(proj, proj, proj, li, b, mx, lirow, brow, conv_w, conv_b, g_head)


def kernel(x, p, g_mix, w_in, conv_w, conv_b, b_igate, b_fgate, g_head, w_pool, pool_scale,
           w_out, g_mlp, w_up, w_down, g_ple, w_ple_gate, b_ple_gate, w_ple, g_final):
    depth = w_in.shape[0]
    h = x.reshape(SEQ, D_MODEL)
    for l in range(depth):
        w_g = jnp.pad(w_in[l, :, D_MAIN:], ((0, 0), (0, LANES - 2 * N_HEADS))).astype(BF16)
        gate_bias = jnp.pad(jnp.concatenate([b_igate[l], b_fgate[l]]), (0, LANES - 2 * N_HEADS)).reshape(1, LANES)

        hn = _rmsnorm(h, g_mix[l], BF16, "norm_mix")
        proj = _proj_in(l, hn, jnp.swapaxes(w_in, 1, 2))
        li, b, mx, lirow, brow = _gates(hn, w_g, gate_bias.astype(F32))
        y_pool = _pool(proj, w_pool[l].astype(BF16), pool_scale[l].reshape(1, D_POOL))
        y_mlstm = _mlstm(proj, li, b, mx, lirow, brow, conv_w[l], conv_b[l].reshape(1, D_QK),
                         g_head[l].reshape(1, D_MLSTM))
        h, hb, ssq = _mix_out(l, y_pool, y_mlstm, w_out, h)

        r = _row_scale(ssq, "scale_mlp")
        act, w_down_b = _mlp_up(l, hb, w_up, g_mlp[l].reshape(D_MODEL, 1), r, w_down)
        h, hb, ssq, w_gate_b = _down(l, act, w_down_b, h, w_ple_gate, g_ple[l].reshape(D_MODEL, 1))

        assert depth == 1
        h = _ple_final(hb, w_gate_b, ssq, b_ple_gate[l].reshape(1, D_MODEL), p[l].reshape(SEQ, D_PLE).astype(BF16),
                       w_ple[l].astype(BF16), h, g_final.reshape(1, D_MODEL))
    return h.reshape(x.shape)
```

```python
import jax
import jax.numpy as jnp
from jax import lax
from jax.experimental import pallas as pl
from jax.experimental.pallas import tpu as pltpu

F32 = jnp.float32
BF16 = jnp.bfloat16

D_MODEL = 4096
SEQ = 8192
D_POOL = 2048
POOL_WINDOWS = (2, 4, 8, 16)
POOL_GROUP = 512
POOL_HALO = 16
D_MLSTM = 2048
N_HEADS = 8
DV = 256
DQK = 128
D_QK = 2 * N_HEADS * DQK
CONV_W = 4
CONV_HALO = 8
CONV_STRIP = 256
GATE_CAP = 15.0
D_FF = 4 * D_MODEL
D_PLE = 256
EPS = 1e-6
D_MAIN = D_POOL + D_QK + 2 * D_MLSTM
LANES = 128
CHUNK = 128
VMEM_LIMIT = 58 * 1024 * 1024


def _params(sem, vmem=VMEM_LIMIT):
    return pltpu.CompilerParams(dimension_semantics=sem, vmem_limit_bytes=vmem)


def _rmsnorm_kernel(x_ref, g_ref, o_ref):
    x = x_ref[...]
    ms = jnp.mean(x * x, axis=-1, keepdims=True)
    o_ref[...] = (x * lax.rsqrt(ms + EPS) * g_ref[...]).astype(o_ref.dtype)


def _rmsnorm(x, g, out_dtype, name, tm=256):
    s, d = x.shape
    return pl.pallas_call(
        _rmsnorm_kernel,
        grid=(s // tm,),
        in_specs=[pl.BlockSpec((tm, d), lambda i: (i, 0)), pl.BlockSpec((1, d), lambda i: (0, 0))],
        out_specs=pl.BlockSpec((tm, d), lambda i: (i, 0)),
        out_shape=jax.ShapeDtypeStruct((s, d), out_dtype),
        compiler_params=_params(("parallel",)),
        name=name,
    )(x, g.reshape(1, d).astype(F32))


def _ws_maps(nb, n_i):
    row = lambda b, i: jnp.where(b > 0, i, 0)
    blk = lambda b: jnp.maximum(b - 1, 0)
    wrow = lambda b, i: jnp.where(b < nb, i, n_i - 1)
    wblk = lambda b: jnp.minimum(b, nb - 1)
    return row, blk, wrow, wblk


def _ws_step(wchunk_ref, g_ref, wbufs, compute):
    b = pl.program_id(0)
    i = pl.program_id(1)

    def stage(buf):
        w = wchunk_ref[...]
        if g_ref is not None:
            w = w * g_ref[...]
        ck = w.shape[0]
        buf[pl.ds(pl.multiple_of(i * ck, ck), ck), :] = w.astype(BF16)

    @pl.when(b == 0)
    def _():
        stage(wbufs[0])

    for parity in (0, 1):
        @pl.when(jnp.logical_and(b > 0, b % 2 == parity))
        def _():
            stage(wbufs[parity])
            compute(wbufs[1 - parity])


def _row_sumsq(h):
    return jnp.broadcast_to(jnp.sum(h * h, axis=1, keepdims=True), (h.shape[0], LANES))


def _lane_tile(r, n):
    return jnp.concatenate([r] * (n // LANES), axis=1)


_PROJ_KINDS = ("f32", "f32", "q", "k", "v", "v", "f32", "f32")
_PROJ_BF16_FIRST = 2
_PROJ_BF16_LAST = 5


def _proj_kernel(a_ref, wchunk_ref, cw_ref, cb_ref, of_ref, ob_ref, wbuf0, wbuf1, ext_ref, halo_ref):
    b = pl.program_id(0)
    i = pl.program_id(1)
    tm = a_ref.shape[0]
    wbufs = (wbuf0, wbuf1)
    n_blocks = len(_PROJ_KINDS)

    def stage(buf):
        w = wchunk_ref[...].T
        ck = w.shape[0]
        buf[pl.ds(pl.multiple_of(i * ck, ck), ck), :] = w.astype(BF16)

    @pl.when(b == 0)
    def _():
        stage(wbuf0)
        halo_ref[...] = jnp.zeros_like(halo_ref)

    for bb in range(1, n_blocks + 1):
        @pl.when(b == bb)
        def _(bb=bb):
            if bb < n_blocks:
                stage(wbufs[bb % 2])
            y = jnp.dot(a_ref[...], wbufs[1 - bb % 2][...], preferred_element_type=F32)
            kind = _PROJ_KINDS[bb - 1]
            if kind == "f32":
                of_ref[...] = y
            elif kind == "v":
                ob_ref[...] = y.astype(BF16)
            else:
                width = ext_ref.shape[1]
                for c in range(y.shape[1] // width):
                    cols = slice(c * width, (c + 1) * width)
                    ext_ref[0:CONV_HALO, :] = jnp.where(i == 0, 0.0, halo_ref[:, cols])
                    ext_ref[CONV_HALO:CONV_HALO + tm, :] = y[:, cols]
                    acc = cb_ref[:, cols]
                    for j in range(CONV_W):
                        off = CONV_HALO - (CONV_W - 1) + j
                        acc = acc + cw_ref[j:j + 1, cols] * ext_ref[off:off + tm, :]
                    acc = acc * jax.nn.sigmoid(acc)
                    if kind == "k":
                        acc = acc * (DQK ** -0.5)
                    ob_ref[:, cols] = acc.astype(BF16)
                halo_ref[...] = y[tm - CONV_HALO:tm, :]


def _mix_out_kernel(yp_ref, ym_ref, wchunk_ref, res_ref, o_ref, ob_ref, ssq_ref, wbuf0, wbuf1):
    def compute(w_ref):
        acc = jnp.dot(yp_ref[...], w_ref[0:D_POOL, :], preferred_element_type=F32)
        acc += jnp.dot(ym_ref[...], w_ref[D_POOL:D_POOL + D_MLSTM, :], preferred_element_type=F32)
        h = res_ref[...] + acc
        o_ref[...] = h
        ob_ref[...] = h.astype(BF16)
        ssq_ref[...] = _row_sumsq(h)
    _ws_step(wchunk_ref, None, (wbuf0, wbuf1), compute)


def _up_kernel(a_ref, wchunk_ref, g_ref, r_ref, wd_ref, o_ref, wdb_ref, wbuf0, wbuf1):
    def compute(w_ref):
        y = jnp.dot(a_ref[...], w_ref[...], preferred_element_type=F32)
        y = jnp.maximum(y * _lane_tile(r_ref[...], y.shape[1]), 0.0)
        o_ref[...] = (y * y).astype(BF16)
        wdb_ref[...] = wd_ref[...].astype(BF16)
    _ws_step(wchunk_ref, g_ref, (wbuf0, wbuf1), compute)


def _ws_call(kern, name, l, a_list, w, k_dim, n_out, tm, tn, extra_in, extra_specs, out_shapes, out_specs):
    m = a_list[0].shape[0]
    nb, n_i = n_out // tn, m // tm
    ck = k_dim // n_i
    row, blk, wrow, wblk = _ws_maps(nb, n_i)
    a_specs = [pl.BlockSpec((tm, a.shape[1]), lambda b, i: (row(b, i), 0)) for a in a_list]
    w_spec = pl.BlockSpec((None, ck, tn), lambda b, i: (l, wrow(b, i), wblk(b)))
    return pl.pallas_call(
        kern,
        grid=(nb + 1, n_i),
        in_specs=a_specs + [w_spec] + extra_specs(row, blk, wrow, wblk),
        out_specs=out_specs(row, blk),
        out_shape=out_shapes,
        scratch_shapes=[pltpu.VMEM((k_dim, tn), BF16), pltpu.VMEM((k_dim, tn), BF16)],
        compiler_params=_params(("arbitrary", "arbitrary")),
        name=name,
    )(*a_list, w, *extra_in)


def _proj_in(l, hn, w_in_t, conv_w, conv_b, tm=1024, tn=1024):
    m = hn.shape[0]
    nb, n_i = len(_PROJ_KINDS), m // tm
    ck = D_MODEL // n_i
    row, blk, wrow, wblk = _ws_maps(nb, n_i)
    n_f32 = sum(kind == "f32" for kind in _PROJ_KINDS)
    n_bf16 = nb - n_f32

    def f32_idx(b, i):
        c = blk(b)
        before = c < _PROJ_BF16_FIRST
        after = c > _PROJ_BF16_LAST
        col = jnp.where(before, c, jnp.where(after, c - n_bf16, _PROJ_BF16_FIRST - 1))
        return jnp.where(jnp.logical_or(before, after), row(b, i), n_i - 1), col

    def bf16_idx(b, i):
        c = blk(b)
        inside = jnp.logical_and(c >= _PROJ_BF16_FIRST, c <= _PROJ_BF16_LAST)
        col = jnp.clip(c - _PROJ_BF16_FIRST, 0, n_bf16 - 1)
        return jnp.where(inside, row(b, i), jnp.where(c < _PROJ_BF16_FIRST, 0, n_i - 1)), col

    conv_col = lambda b, i: (0, jnp.clip(blk(b) - _PROJ_BF16_FIRST, 0, D_QK // tn - 1))
    return pl.pallas_call(
        _proj_kernel,
        grid=(nb + 1, n_i),
        in_specs=[
            pl.BlockSpec((tm, D_MODEL), lambda b, i: (row(b, i), 0)),
            pl.BlockSpec((None, tn, ck), lambda b, i: (l, wblk(b), wrow(b, i))),
            pl.BlockSpec((CONV_W, tn), conv_col),
            pl.BlockSpec((1, tn), conv_col),
        ],
        out_specs=[pl.BlockSpec((tm, tn), f32_idx), pl.BlockSpec((tm, tn), bf16_idx)],
        out_shape=[jax.ShapeDtypeStruct((m, n_f32 * tn), F32), jax.ShapeDtypeStruct((m, n_bf16 * tn), BF16)],
        scratch_shapes=[pltpu.VMEM((D_MODEL, tn), BF16), pltpu.VMEM((D_MODEL, tn), BF16),
                        pltpu.VMEM((tm + CONV_HALO, CONV_STRIP), F32), pltpu.VMEM((CONV_HALO, tn), F32)],
        compiler_params=_params(("arbitrary", "arbitrary")),
        name="proj_in",
    )(hn, w_in_t, conv_w, conv_b)


def _mix_out(l, y_pool, y_mlstm, w_out, resid, tm=512, tn=1024):
    m = resid.shape[0]
    nb = D_MODEL // tn
    tile = lambda row, blk: pl.BlockSpec((tm, tn), lambda b, i: (row(b, i), blk(b)))
    return _ws_call(
        _mix_out_kernel, "mix_out", l, [y_pool, y_mlstm], w_out, D_MODEL, D_MODEL, tm, tn, [resid],
        lambda row, blk, wrow, wblk: [tile(row, blk)],
        [jax.ShapeDtypeStruct((m, D_MODEL), F32), jax.ShapeDtypeStruct((m, D_MODEL), BF16),
         jax.ShapeDtypeStruct((m, nb * LANES), F32)],
        lambda row, blk: [tile(row, blk), tile(row, blk),
                          pl.BlockSpec((tm, LANES), lambda b, i: (row(b, i), blk(b)))])


def _mlp_up(l, hb, w_up, g, r, w_down, tm=1024, tn=1024):
    m = hb.shape[0]
    nb, n_i = D_FF // tn, m // tm
    ck = D_MODEL // n_i
    wd_rows = D_FF // (nb * n_i)
    side = lambda row, blk: (lambda b, i: (blk(b) * n_i + row(b, i), 0))
    return _ws_call(
        _up_kernel, "mlp_up", l, [hb], w_up, D_MODEL, D_FF, tm, tn, [g, r, w_down],
        lambda row, blk, wrow, wblk: [
            pl.BlockSpec((ck, 1), lambda b, i: (wrow(b, i), 0)),
            pl.BlockSpec((tm, LANES), lambda b, i: (row(b, i), 0)),
            pl.BlockSpec((None, wd_rows, D_MODEL), lambda b, i: (l,) + side(row, blk)(b, i)),
        ],
        [jax.ShapeDtypeStruct((m, D_FF), BF16), jax.ShapeDtypeStruct((D_FF, D_MODEL), BF16)],
        lambda row, blk: [pl.BlockSpec((tm, tn), lambda b, i: (row(b, i), blk(b))),
                          pl.BlockSpec((wd_rows, D_MODEL), side(row, blk))])


def _ple_kernel(a_ref, w_ref, ssq_ref, bias_ref, p_ref, wp_ref, res_ref, gf_ref, o_ref):
    j = pl.program_id(1)
    tn = w_ref.shape[1]
    n_j = o_ref.shape[1] // tn

    def tile():
        s = ssq_ref[:, 0:LANES]
        for c in range(1, ssq_ref.shape[1] // LANES):
            s = s + ssq_ref[:, c * LANES:(c + 1) * LANES]
        r = lax.rsqrt(s * (1.0 / D_MODEL) + EPS)
        z = jnp.dot(a_ref[...], w_ref[...], preferred_element_type=F32)
        z = z * _lane_tile(r, tn) + bias_ref[...]
        pe = jnp.dot(p_ref[...], wp_ref[...], preferred_element_type=F32)
        return res_ref[...] + jax.nn.sigmoid(z) * pe

    for jj in range(n_j):
        @pl.when(j == jj)
        def _():
            o_ref[:, jj * tn:(jj + 1) * tn] = tile()
            if jj == n_j - 1:
                x = o_ref[...]
                ms = jnp.mean(x * x, axis=-1, keepdims=True)
                o_ref[...] = x * lax.rsqrt(ms + EPS) * gf_ref[...]


def _ple_final(hb, w_gate_b, ssq, b_gate, p, w_ple, resid, g_final, tm=512, tn=1024):
    m = hb.shape[0]
    return pl.pallas_call(
        _ple_kernel,
        grid=(m // tm, D_MODEL // tn),
        in_specs=[
            pl.BlockSpec((tm, D_MODEL), lambda i, j: (i, 0)),
            pl.BlockSpec((D_MODEL, tn), lambda i, j: (0, j)),
            pl.BlockSpec((tm, ssq.shape[1]), lambda i, j: (i, 0)),
            pl.BlockSpec((1, tn), lambda i, j: (0, j)),
            pl.BlockSpec((tm, D_PLE), lambda i, j: (i, 0)),
            pl.BlockSpec((D_PLE, tn), lambda i, j: (0, j)),
            pl.BlockSpec((tm, tn), lambda i, j: (i, j)),
            pl.BlockSpec((1, D_MODEL), lambda i, j: (0, 0)),
        ],
        out_specs=pl.BlockSpec((tm, D_MODEL), lambda i, j: (i, 0)),
        out_shape=jax.ShapeDtypeStruct((m, D_MODEL), F32),
        compiler_params=_params(("parallel", "arbitrary")),
        name="ple_final",
    )(hb, w_gate_b, ssq, b_gate, p, w_ple, resid, g_final)


def _down_kernel(a_ref, w_ref, res_hbm, wg_ref, g_ref, o_ref, ob_ref, ssq_ref, wgb_ref, res_sem):
    i, j, k = pl.program_id(0), pl.program_id(1), pl.program_id(2)
    last = pl.num_programs(2) - 1
    tm, tn = o_ref.shape
    wgb_ref[...] = (wg_ref[...] * g_ref[...]).astype(BF16)

    def part():
        return jnp.dot(a_ref[...], w_ref[...], preferred_element_type=F32)

    @pl.when(k == 0)
    def _():
        rows = pl.ds(pl.multiple_of(i * tm, tm), tm)
        cols = pl.ds(pl.multiple_of(j * tn, tn), tn)
        res_copy = pltpu.make_async_copy(res_hbm.at[rows, cols], o_ref, res_sem)
        res_copy.start()
        p = part()
        res_copy.wait()
        o_ref[...] += p

    @pl.when(jnp.logical_and(k != 0, k != last))
    def _():
        o_ref[...] += part()

    @pl.when(k == last)
    def _():
        h = o_ref[...] + part()
        o_ref[...] = h
        ob_ref[...] = h.astype(BF16)
        ssq_ref[...] = _row_sumsq(h)


def _down(l, a, w, resid, w_gate, g_gate, tm=1024, tn=1024, tk=4096):
    m, k = a.shape
    n = w.shape[1]
    n_j, n_k = n // tn, k // tk
    side_rows = w_gate.shape[1] // ((m // tm) * n_j * n_k)
    tile = pl.BlockSpec((tm, tn), lambda i, j, kk: (i, j))
    step = lambda i, j, kk: (i * n_j + j) * n_k + kk
    return pl.pallas_call(
        _down_kernel,
        grid=(m // tm, n_j, n_k),
        in_specs=[
            pl.BlockSpec((tm, tk), lambda i, j, kk: (i, kk)),
            pl.BlockSpec((tk, tn), lambda i, j, kk: (kk, j)),
            pl.BlockSpec(memory_space=pl.ANY),
            pl.BlockSpec((None, side_rows, w_gate.shape[2]), lambda i, j, kk: (l, step(i, j, kk), 0)),
            pl.BlockSpec((side_rows, 1), lambda i, j, kk: (step(i, j, kk), 0)),
        ],
        out_specs=[tile, tile, pl.BlockSpec((tm, LANES), lambda i, j, kk: (i, j)),
                   pl.BlockSpec((side_rows, w_gate.shape[2]), lambda i, j, kk: (step(i, j, kk), 0))],
        out_shape=[jax.ShapeDtypeStruct((m, n), F32), jax.ShapeDtypeStruct((m, n), BF16),
                   jax.ShapeDtypeStruct((m, n_j * LANES), F32),
                   jax.ShapeDtypeStruct(w_gate.shape[1:], BF16)],
        scratch_shapes=[pltpu.SemaphoreType.DMA(())],
        compiler_params=_params(("arbitrary", "arbitrary", "arbitrary")),
        name="mlp_down",
    )(a, w, resid, w_gate, g_gate)


def _row_scale_kernel(ssq_ref, r_ref):
    s = ssq_ref[:, 0:LANES]
    for j in range(1, ssq_ref.shape[1] // LANES):
        s = s + ssq_ref[:, j * LANES:(j + 1) * LANES]
    r_ref[...] = lax.rsqrt(s * (1.0 / D_MODEL) + EPS)


def _row_scale(ssq, name, tm=1024):
    m, w = ssq.shape
    return pl.pallas_call(
        _row_scale_kernel,
        grid=(m // tm,),
        in_specs=[pl.BlockSpec((tm, w), lambda i: (i, 0))],
        out_specs=pl.BlockSpec((tm, LANES), lambda i: (i, 0)),
        out_shape=jax.ShapeDtypeStruct((m, LANES), F32),
        compiler_params=_params(("parallel",)),
        name=name,
    )(ssq)


def _chunk_scan(x, pos, combine, fill):
    sh = 1
    while sh < CHUNK:
        prev = jnp.where(pos >= sh, pltpu.roll(x, sh, 0), fill)
        x = combine(x, prev)
        sh *= 2
    return x


def _gates_kernel(a_ref, w_ref, bias_ref, li_ref, b_ref, mx_ref, lirow_ref, brow_ref):
    tm = a_ref.shape[0]
    pre = jnp.dot(a_ref[...], w_ref[...], preferred_element_type=F32) + bias_ref[...]
    pre = GATE_CAP * jnp.tanh(pre / GATE_CAP)
    logf = jnp.minimum(pre, 0.0) - jnp.log1p(jnp.exp(-jnp.abs(pre)))
    pos = lax.broadcasted_iota(jnp.int32, (tm, LANES), 0) % CHUNK
    b = _chunk_scan(logf, pos, jnp.add, 0.0)
    b = pltpu.roll(b, LANES - N_HEADS, 1)
    li = pre
    mx = b + _chunk_scan(li - b, pos, jnp.maximum, -jnp.inf)
    li_ref[...] = li
    b_ref[...] = b
    mx_ref[...] = mx
    for c in range(tm // CHUNK):
        rows = slice(c * CHUNK, (c + 1) * CHUNK)
        lirow_ref[c] = li[rows].T[:N_HEADS]
        brow_ref[c] = b[rows].T[:N_HEADS]


def _gates(hn, w_g, bias, tm=1024):
    m, k = hn.shape
    col = jax.ShapeDtypeStruct((m, LANES), F32)
    row = jax.ShapeDtypeStruct((m // CHUNK, N_HEADS, CHUNK), F32)
    col_spec = pl.BlockSpec((tm, LANES), lambda i: (i, 0))
    row_spec = pl.BlockSpec((tm // CHUNK, N_HEADS, CHUNK), lambda i: (i, 0, 0))
    return pl.pallas_call(
        _gates_kernel,
        grid=(m // tm,),
        in_specs=[
            pl.BlockSpec((tm, k), lambda i: (i, 0)),
            pl.BlockSpec((k, LANES), lambda i: (0, 0)),
            pl.BlockSpec((1, LANES), lambda i: (0, 0)),
        ],
        out_specs=[col_spec, col_spec, col_spec, row_spec, row_spec],
        out_shape=[col, col, col, row, row],
        compiler_params=_params(("parallel",)),
        name="gates",
    )(hn, w_g, bias)


def _pool_kernel(u_ref, w_ref, scale_ref, o_ref, ext_ref):
    ts = u_ref.shape[0]
    i = pl.program_id(0)

    @pl.when(i == 0)
    def _():
        ext_ref[0:POOL_HALO, :] = jnp.zeros((POOL_HALO, D_POOL), F32)

    @pl.when(i != 0)
    def _():
        ext_ref[0:POOL_HALO, :] = ext_ref[ts:ts + POOL_HALO, :]

    ext_ref[POOL_HALO:POOL_HALO + ts, :] = u_ref[...]
    t = i * ts + lax.broadcasted_iota(jnp.int32, (ts, 1), 0)
    for g, win in enumerate(POOL_WINDOWS):
        cols = slice(g * POOL_GROUP, (g + 1) * POOL_GROUP)
        u = ext_ref[POOL_HALO:POOL_HALO + ts, cols]
        win_sum = u
        for j in range(1, win):
            win_sum = win_sum + ext_ref[POOL_HALO - j:POOL_HALO - j + ts, cols]
        cnt = jnp.minimum(t + 1, win).astype(F32)
        z = (win_sum / cnt - u).astype(BF16)
        y = jnp.dot(z, w_ref[g], preferred_element_type=F32)
        o_ref[:, cols] = (y * scale_ref[:, cols]).astype(o_ref.dtype)


def _pool(proj, w_pool, scale, ts=512):
    s = proj.shape[0]
    return pl.pallas_call(
        _pool_kernel,
        grid=(s // ts,),
        in_specs=[
            pl.BlockSpec((ts, D_POOL), lambda i: (i, 0)),
            pl.BlockSpec((len(POOL_WINDOWS), POOL_GROUP, POOL_GROUP), lambda i: (0, 0, 0)),
            pl.BlockSpec((1, D_POOL), lambda i: (0, 0)),
        ],
        out_specs=pl.BlockSpec((ts, D_POOL), lambda i: (i, 0)),
        out_shape=jax.ShapeDtypeStruct((s, D_POOL), BF16),
        scratch_shapes=[pltpu.VMEM((ts + POOL_HALO, D_POOL), F32)],
        compiler_params=_params(("arbitrary",)),
        name="pool_mixer",
    )(proj, w_pool, scale)


def _mlstm_kernel(q_ref, k_ref, v_ref, o_ref, li_ref, b_ref, mx_ref, lirow_ref, brow_ref, gh_ref,
                  y_ref, c_ref, n_ref, m_ref):
    L = CHUNK

    @pl.when(pl.program_id(0) == 0)
    def _():
        c_ref[...] = jnp.zeros_like(c_ref)
        n_ref[...] = jnp.zeros_like(n_ref)
        m_ref[...] = jnp.zeros_like(m_ref)

    t_idx = lax.broadcasted_iota(jnp.int32, (L, L), 0)
    s_idx = lax.broadcasted_iota(jnp.int32, (L, L), 1)
    causal = s_idx <= t_idx

    for h in range(N_HEADS):
        qb = q_ref[:, h * DQK:(h + 1) * DQK]
        kb = k_ref[:, h * DQK:(h + 1) * DQK]
        vcols = slice(h * DV, (h + 1) * DV)
        v = v_ref[:, vcols]
        li_col = li_ref[:, h:h + 1]
        b_col = b_ref[:, h:h + 1]
        mx_col = mx_ref[:, h:h + 1]
        li_row = lirow_ref[0, h:h + 1, :]
        b_row = brow_ref[0, h:h + 1, :]
        m_prev = m_ref[h][:, 0:1]
        c_prev = c_ref[h]
        n_prev = n_ref[h]

        d_mat = jnp.where(causal, b_col - b_row + li_row, -jnp.inf)
        e_col = b_col + m_prev
        m_t = jnp.maximum(mx_col, e_col)
        s_qk = lax.dot_general(qb, kb, (((1,), (1,)), ((), ())), preferred_element_type=F32)
        w_mat = jnp.exp(d_mat - m_t) * s_qk
        w_inter = jnp.exp(e_col - m_t)
        num = jnp.dot(w_mat.astype(BF16), v, preferred_element_type=F32)
        num = num + w_inter * jnp.dot(qb, c_prev.astype(BF16), preferred_element_type=F32)
        den = jnp.sum(w_mat, axis=-1, keepdims=True)
        den = den + w_inter * jnp.sum(qb.astype(F32) * n_prev, axis=-1, keepdims=True)
        inv = 1.0 / jnp.maximum(jnp.abs(den), jnp.exp(-m_t))
        msq = jnp.mean(num * num, axis=-1, keepdims=True)
        scale = inv * lax.rsqrt(inv * inv * msq + EPS)
        hh = num * scale * (gh_ref[:, vcols] * jax.nn.sigmoid(o_ref[:, vcols]))
        y_ref[:, vcols] = hh.astype(y_ref.dtype)

        g_tot = b_col[L - 1:L, :]
        a_col = g_tot + li_col - b_col
        m_loc = jnp.max(a_col, axis=0, keepdims=True)
        kw = kb.astype(F32) * jnp.exp(a_col - m_loc)
        c_loc = lax.dot_general(kw.astype(BF16), v, (((0,), (0,)), ((), ())), preferred_element_type=F32)
        n_loc = jnp.sum(kw, axis=0, keepdims=True)
        m_new = jnp.maximum(g_tot + m_prev, m_loc)
        s_old = jnp.exp(g_tot + m_prev - m_new)
        s_loc = jnp.exp(m_loc - m_new)
        c_ref[h] = s_old * c_prev + s_loc * c_loc
        n_ref[h] = s_old * n_prev + s_loc * n_loc
        m_ref[h] = jnp.broadcast_to(m_new, (1, LANES))


def _mlstm(proj_f32, proj_bf16, li, b, mx, lirow, brow, g_head):
    s = proj_f32.shape[0]
    L = CHUNK
    half = N_HEADS * DQK
    col_spec = pl.BlockSpec((L, LANES), lambda i: (i, 0))
    row_spec = pl.BlockSpec((1, N_HEADS, L), lambda i: (i, 0, 0))
    return pl.pallas_call(
        _mlstm_kernel,
        grid=(s // L,),
        in_specs=[
            pl.BlockSpec((L, half), lambda i: (i, 0)),
            pl.BlockSpec((L, half), lambda i: (i, 1)),
            pl.BlockSpec((L, D_MLSTM), lambda i: (i, D_QK // D_MLSTM)),
            pl.BlockSpec((L, D_MLSTM), lambda i: (i, D_POOL // D_MLSTM)),
            col_spec, col_spec, col_spec, row_spec, row_spec,
            pl.BlockSpec((1, D_MLSTM), lambda i: (0, 0)),
        ],
        out_specs=pl.BlockSpec((L, D_MLSTM), lambda i: (i, 0)),
        out_shape=jax.ShapeDtypeStruct((s, D_MLSTM), BF16),
        scratch_shapes=[
            pltpu.VMEM((N_HEADS, DQK, DV), F32),
            pltpu.VMEM((N_HEADS, 1, DQK), F32),
            pltpu.VMEM((N_HEADS, 1, LANES), F32),
        ],
        compiler_params=_params(("arbitrary",)),
        name="mlstm_mixer",
    )(proj_bf16, proj_bf16, proj_bf16, proj_f32, li, b, mx, lirow, brow, g_head)


def kernel(x, p, g_mix, w_in, conv_w, conv_b, b_igate, b_fgate, g_head, w_pool, pool_scale,
           w_out, g_mlp, w_up, w_down, g_ple, w_ple_gate, b_ple_gate, w_ple, g_final):
    depth = w_in.shape[0]
    h = x.reshape(SEQ, D_MODEL)
    for l in range(depth):
        w_g = jnp.pad(w_in[l, :, D_MAIN:], ((0, 0), (0, LANES - 2 * N_HEADS))).astype(BF16)
        gate_bias = jnp.pad(jnp.concatenate([b_igate[l], b_fgate[l]]), (0, LANES - 2 * N_HEADS)).reshape(1, LANES)

        hn = _rmsnorm(h, g_mix[l], BF16, "norm_mix")
        proj_f32, proj_bf16 = _proj_in(l, hn, jnp.swapaxes(w_in, 1, 2), conv_w[l], conv_b[l].reshape(1, D_QK))
        li, b, mx, lirow, brow = _gates(hn, w_g, gate_bias.astype(F32))
        y_pool = _pool(proj_f32, w_pool[l].astype(BF16), pool_scale[l].reshape(1, D_POOL))
        y_mlstm = _mlstm(proj_f32, proj_bf16, li, b, mx, lirow, brow, g_head[l].reshape(1, D_MLSTM))
        h, hb, ssq = _mix_out(l, y_pool, y_mlstm, w_out, h)

        r = _row_scale(ssq, "scale_mlp")
        act, w_down_b = _mlp_up(l, hb, w_up, g_mlp[l].reshape(D_MODEL, 1), r, w_down)
        h, hb, ssq, w_gate_b = _down(l, act, w_down_b, h, w_ple_gate, g_ple[l].reshape(D_MODEL, 1))

        assert depth == 1
        h = _ple_final(hb, w_gate_b, ssq, b_ple_gate[l].reshape(1, D_MODEL), p[l].reshape(SEQ, D_PLE).astype(BF16),
                       w_ple[l].astype(BF16), h, g_final.reshape(1, D_MODEL))
    return h.reshape(x.shape)
```

```python
import jax
import jax.numpy as jnp
from jax import lax
from jax.experimental import pallas as pl
from jax.experimental.pallas import tpu as pltpu

F32 = jnp.float32
BF16 = jnp.bfloat16

D_MODEL = 4096
SEQ = 8192
D_POOL = 2048
POOL_WINDOWS = (2, 4, 8, 16)
POOL_GROUP = 512
POOL_HALO = 16
D_MLSTM = 2048
N_HEADS = 8
DV = 256
DQK = 128
D_QK = 2 * N_HEADS * DQK
CONV_W = 4
CONV_HALO = 8
GATE_CAP = 15.0
D_FF = 4 * D_MODEL
D_PLE = 256
EPS = 1e-6
D_MAIN = D_POOL + D_QK + 2 * D_MLSTM
LANES = 128
CHUNK = 256
VMEM_LIMIT = 58 * 1024 * 1024


def _params(sem, vmem=VMEM_LIMIT):
    return pltpu.CompilerParams(dimension_semantics=sem, vmem_limit_bytes=vmem)


def _chunk_scan(x, pos, combine, fill):
    sh = 1
    while sh < CHUNK:
        prev = jnp.where(pos >= sh, pltpu.roll(x, sh, 0), fill)
        x = combine(x, prev)
        sh *= 2
    return x


def _norm_gates_kernel(x_ref, g_ref, w_ref, bias_ref, hn_ref, li_ref, b_ref, mx_ref, lirow_ref, brow_ref):
    tm = x_ref.shape[0]
    x = x_ref[...]
    ms = jnp.mean(x * x, axis=-1, keepdims=True)
    hn = (x * lax.rsqrt(ms + EPS) * g_ref[...]).astype(BF16)
    hn_ref[...] = hn
    pre = jnp.dot(hn, w_ref[...], preferred_element_type=F32) + bias_ref[...]
    pre = GATE_CAP * jnp.tanh(pre / GATE_CAP)
    logf = jnp.minimum(pre, 0.0) - jnp.log1p(jnp.exp(-jnp.abs(pre)))
    pos = lax.broadcasted_iota(jnp.int32, (tm, LANES), 0) % CHUNK
    b = _chunk_scan(logf, pos, jnp.add, 0.0)
    b = pltpu.roll(b, LANES - N_HEADS, 1)
    li = pre
    mx = b + _chunk_scan(li - b, pos, jnp.maximum, -jnp.inf)
    li_ref[...] = li
    b_ref[...] = b
    mx_ref[...] = mx
    for c in range(tm // CHUNK):
        rows = slice(c * CHUNK, (c + 1) * CHUNK)
        lirow_ref[c] = li[rows].T[:N_HEADS]
        brow_ref[c] = b[rows].T[:N_HEADS]


def _norm_gates(x, g, w_g, bias, tm=512):
    m, d = x.shape
    col = jax.ShapeDtypeStruct((m, LANES), F32)
    row = jax.ShapeDtypeStruct((m // CHUNK, N_HEADS, CHUNK), F32)
    col_spec = pl.BlockSpec((tm, LANES), lambda i: (i, 0))
    row_spec = pl.BlockSpec((tm // CHUNK, N_HEADS, CHUNK), lambda i: (i, 0, 0))
    return pl.pallas_call(
        _norm_gates_kernel,
        grid=(m // tm,),
        in_specs=[
            pl.BlockSpec((tm, d), lambda i: (i, 0)),
            pl.BlockSpec((1, d), lambda i: (0, 0)),
            pl.BlockSpec((d, LANES), lambda i: (0, 0)),
            pl.BlockSpec((1, LANES), lambda i: (0, 0)),
        ],
        out_specs=[pl.BlockSpec((tm, d), lambda i: (i, 0)), col_spec, col_spec, col_spec, row_spec, row_spec],
        out_shape=[jax.ShapeDtypeStruct((m, d), BF16), col, col, col, row, row],
        compiler_params=_params(("parallel",)),
        name="norm_gates",
    )(x, g, w_g, bias)


def _ws_maps(nb, n_i):
    row = lambda b, i: jnp.where(b > 0, i, 0)
    blk = lambda b: jnp.maximum(b - 1, 0)
    wrow = lambda b, i: jnp.where(b < nb, i, n_i - 1)
    wblk = lambda b: jnp.minimum(b, nb - 1)
    return row, blk, wrow, wblk


def _ws_step(wchunk_ref, g_ref, wbufs, compute):
    b = pl.program_id(0)
    i = pl.program_id(1)

    def stage(buf):
        w = wchunk_ref[...]
        if g_ref is not None:
            w = w * g_ref[...]
        ck = w.shape[0]
        buf[pl.ds(pl.multiple_of(i * ck, ck), ck), :] = w.astype(BF16)

    @pl.when(b == 0)
    def _():
        stage(wbufs[0])

    for parity in (0, 1):
        @pl.when(jnp.logical_and(b > 0, b % 2 == parity))
        def _():
            stage(wbufs[parity])
            compute(wbufs[1 - parity])


def _row_sumsq(h):
    return jnp.broadcast_to(jnp.sum(h * h, axis=1, keepdims=True), (h.shape[0], LANES))


def _lane_tile(r, n):
    return jnp.concatenate([r] * (n // LANES), axis=1)


def _proj_kernel(a_ref, wchunk_ref, o_ref, wbuf_ref):
    b = pl.program_id(0)
    i = pl.program_id(1)

    def stage():
        w = wchunk_ref[...].T
        ck = w.shape[0]
        wbuf_ref[b % 2, pl.ds(pl.multiple_of(i * ck, ck), ck), :] = w.astype(BF16)

    @pl.when(b == 0)
    def _():
        stage()

    @pl.when(b > 0)
    def _():
        stage()
        o_ref[...] = jnp.dot(a_ref[...], wbuf_ref[(b + 1) % 2], preferred_element_type=F32)

def _mix_out_kernel(yp_ref, ym_ref, wchunk_ref, res_ref, o_ref, ob_ref, ssq_ref, wbuf0, wbuf1):
    def compute(w_ref):
        acc = jnp.dot(yp_ref[...], w_ref[0:D_POOL, :], preferred_element_type=F32)
        acc += jnp.dot(ym_ref[...], w_ref[D_POOL:D_POOL + D_MLSTM, :], preferred_element_type=F32)
        h = res_ref[...] + acc
        o_ref[...] = h
        ob_ref[...] = h.astype(BF16)
        ssq_ref[...] = _row_sumsq(h)
    _ws_step(wchunk_ref, None, (wbuf0, wbuf1), compute)


def _up_kernel(a_ref, wchunk_ref, g_ref, r_ref, wd_ref, o_ref, wdb_ref, wbuf0, wbuf1):
    def compute(w_ref):
        y = jnp.dot(a_ref[...], w_ref[...], preferred_element_type=F32)
        y = jnp.maximum(y * _lane_tile(r_ref[...], y.shape[1]), 0.0)
        o_ref[...] = (y * y).astype(BF16)
        wdb_ref[...] = wd_ref[...].astype(BF16)
    _ws_step(wchunk_ref, g_ref, (wbuf0, wbuf1), compute)


def _ws_call(kern, name, l, a_list, w, k_dim, n_out, tm, tn, extra_in, extra_specs, out_shapes, out_specs):
    m = a_list[0].shape[0]
    nb, n_i = n_out // tn, m // tm
    ck = k_dim // n_i
    row, blk, wrow, wblk = _ws_maps(nb, n_i)
    a_specs = [pl.BlockSpec((tm, a.shape[1]), lambda b, i: (row(b, i), 0)) for a in a_list]
    w_spec = pl.BlockSpec((None, ck, tn), lambda b, i: (l, wrow(b, i), wblk(b)))
    return pl.pallas_call(
        kern,
        grid=(nb + 1, n_i),
        in_specs=a_specs + [w_spec] + extra_specs(row, blk, wrow, wblk),
        out_specs=out_specs(row, blk),
        out_shape=out_shapes,
        scratch_shapes=[pltpu.VMEM((k_dim, tn), BF16), pltpu.VMEM((k_dim, tn), BF16)],
        compiler_params=_params(("arbitrary", "arbitrary")),
        name=name,
    )(*a_list, w, *extra_in)


def _proj_in(l, hn, w_in_t, tm=1024, tn=1024):
    m = hn.shape[0]
    nb, n_i = D_MAIN // tn, m // tm
    ck = D_MODEL // n_i
    row, blk, wrow, wblk = _ws_maps(nb, n_i)
    return pl.pallas_call(
        _proj_kernel,
        grid=(nb + 1, n_i),
        in_specs=[
            pl.BlockSpec((tm, D_MODEL), lambda b, i: (row(b, i), 0)),
            pl.BlockSpec((None, tn, ck), lambda b, i: (l, wblk(b), wrow(b, i))),
        ],
        out_specs=pl.BlockSpec((tm, tn), lambda b, i: (row(b, i), blk(b))),
        out_shape=jax.ShapeDtypeStruct((m, D_MAIN), F32),
        scratch_shapes=[pltpu.VMEM((2, D_MODEL, tn), BF16)],
        compiler_params=_params(("arbitrary", "arbitrary")),
        name="proj_in",
    )(hn, w_in_t)


def _mix_out(l, y_pool, y_mlstm, w_out, resid, tm=512, tn=1024):
    m = resid.shape[0]
    nb = D_MODEL // tn
    tile = lambda row, blk: pl.BlockSpec((tm, tn), lambda b, i: (row(b, i), blk(b)))
    return _ws_call(
        _mix_out_kernel, "mix_out", l, [y_pool, y_mlstm], w_out, D_MODEL, D_MODEL, tm, tn, [resid],
        lambda row, blk, wrow, wblk: [tile(row, blk)],
        [jax.ShapeDtypeStruct((m, D_MODEL), F32), jax.ShapeDtypeStruct((m, D_MODEL), BF16),
         jax.ShapeDtypeStruct((m, nb * LANES), F32)],
        lambda row, blk: [tile(row, blk), tile(row, blk),
                          pl.BlockSpec((tm, LANES), lambda b, i: (row(b, i), blk(b)))])


def _mlp_up(l, hb, w_up, g, r, w_down, tm=1024, tn=1024):
    m = hb.shape[0]
    nb, n_i = D_FF // tn, m // tm
    ck = D_MODEL // n_i
    wd_rows = D_FF // (nb * n_i)
    side = lambda row, blk: (lambda b, i: (blk(b) * n_i + row(b, i), 0))
    return _ws_call(
        _up_kernel, "mlp_up", l, [hb], w_up, D_MODEL, D_FF, tm, tn, [g, r, w_down],
        lambda row, blk, wrow, wblk: [
            pl.BlockSpec((ck, 1), lambda b, i: (wrow(b, i), 0)),
            pl.BlockSpec((tm, LANES), lambda b, i: (row(b, i), 0)),
            pl.BlockSpec((None, wd_rows, D_MODEL), lambda b, i: (l,) + side(row, blk)(b, i)),
        ],
        [jax.ShapeDtypeStruct((m, D_FF), BF16), jax.ShapeDtypeStruct((D_FF, D_MODEL), BF16)],
        lambda row, blk: [pl.BlockSpec((tm, tn), lambda b, i: (row(b, i), blk(b))),
                          pl.BlockSpec((wd_rows, D_MODEL), side(row, blk))])


def _ple_kernel(a_ref, w_ref, ssq_ref, bias_ref, p_ref, wp_ref, res_ref, gf_ref, o_ref):
    j = pl.program_id(1)
    tn = w_ref.shape[1]
    n_j = o_ref.shape[1] // tn

    def tile():
        s = ssq_ref[:, 0:LANES]
        for c in range(1, ssq_ref.shape[1] // LANES):
            s = s + ssq_ref[:, c * LANES:(c + 1) * LANES]
        r = lax.rsqrt(s * (1.0 / D_MODEL) + EPS)
        z = jnp.dot(a_ref[...], w_ref[...], preferred_element_type=F32)
        z = z * _lane_tile(r, tn) + bias_ref[...]
        pe = jnp.dot(p_ref[...], wp_ref[...], preferred_element_type=F32)
        return res_ref[...] + jax.nn.sigmoid(z) * pe

    for jj in range(n_j):
        @pl.when(j == jj)
        def _():
            o_ref[:, jj * tn:(jj + 1) * tn] = tile()
            if jj == n_j - 1:
                x = o_ref[...]
                ms = jnp.mean(x * x, axis=-1, keepdims=True)
                o_ref[...] = x * lax.rsqrt(ms + EPS) * gf_ref[...]


def _ple_final(hb, w_gate_b, ssq, b_gate, p, w_ple, resid, g_final, tm=512, tn=1024):
    m = hb.shape[0]
    return pl.pallas_call(
        _ple_kernel,
        grid=(m // tm, D_MODEL // tn),
        in_specs=[
            pl.BlockSpec((tm, D_MODEL), lambda i, j: (i, 0)),
            pl.BlockSpec((D_MODEL, tn), lambda i, j: (0, j)),
            pl.BlockSpec((tm, ssq.shape[1]), lambda i, j: (i, 0)),
            pl.BlockSpec((1, tn), lambda i, j: (0, j)),
            pl.BlockSpec((tm, D_PLE), lambda i, j: (i, 0)),
            pl.BlockSpec((D_PLE, tn), lambda i, j: (0, j)),
            pl.BlockSpec((tm, tn), lambda i, j: (i, j)),
            pl.BlockSpec((1, D_MODEL), lambda i, j: (0, 0)),
        ],
        out_specs=pl.BlockSpec((tm, D_MODEL), lambda i, j: (i, 0)),
        out_shape=jax.ShapeDtypeStruct((m, D_MODEL), F32),
        compiler_params=_params(("parallel", "arbitrary")),
        name="ple_final",
    )(hb, w_gate_b, ssq, b_gate, p, w_ple, resid, g_final)


def _down_kernel(a_ref, w_ref, res_hbm, wg_ref, g_ref, o_ref, ob_ref, ssq_ref, wgb_ref, res_sem):
    i, j, k = pl.program_id(0), pl.program_id(1), pl.program_id(2)
    last = pl.num_programs(2) - 1
    tm, tn = o_ref.shape
    wgb_ref[...] = (wg_ref[...] * g_ref[...]).astype(BF16)

    def part():
        return jnp.dot(a_ref[...], w_ref[...], preferred_element_type=F32)

    @pl.when(k == 0)
    def _():
        rows = pl.ds(pl.multiple_of(i * tm, tm), tm)
        cols = pl.ds(pl.multiple_of(j * tn, tn), tn)
        res_copy = pltpu.make_async_copy(res_hbm.at[rows, cols], o_ref, res_sem)
        res_copy.start()
        p = part()
        res_copy.wait()
        o_ref[...] += p

    @pl.when(jnp.logical_and(k != 0, k != last))
    def _():
        o_ref[...] += part()

    @pl.when(k == last)
    def _():
        h = o_ref[...] + part()
        o_ref[...] = h
        ob_ref[...] = h.astype(BF16)
        ssq_ref[...] = _row_sumsq(h)


def _down(l, a, w, resid, w_gate, g_gate, tm=1024, tn=1024, tk=4096):
    m, k = a.shape
    n = w.shape[1]
    n_j, n_k = n // tn, k // tk
    side_rows = w_gate.shape[1] // ((m // tm) * n_j * n_k)
    tile = pl.BlockSpec((tm, tn), lambda i, j, kk: (i, j))
    step = lambda i, j, kk: (i * n_j + j) * n_k + kk
    return pl.pallas_call(
        _down_kernel,
        grid=(m // tm, n_j, n_k),
        in_specs=[
            pl.BlockSpec((tm, tk), lambda i, j, kk: (i, kk)),
            pl.BlockSpec((tk, tn), lambda i, j, kk: (kk, j)),
            pl.BlockSpec(memory_space=pl.ANY),
            pl.BlockSpec((None, side_rows, w_gate.shape[2]), lambda i, j, kk: (l, step(i, j, kk), 0)),
            pl.BlockSpec((side_rows, 1), lambda i, j, kk: (step(i, j, kk), 0)),
        ],
        out_specs=[tile, tile, pl.BlockSpec((tm, LANES), lambda i, j, kk: (i, j)),
                   pl.BlockSpec((side_rows, w_gate.shape[2]), lambda i, j, kk: (step(i, j, kk), 0))],
        out_shape=[jax.ShapeDtypeStruct((m, n), F32), jax.ShapeDtypeStruct((m, n), BF16),
                   jax.ShapeDtypeStruct((m, n_j * LANES), F32),
                   jax.ShapeDtypeStruct(w_gate.shape[1:], BF16)],
        scratch_shapes=[pltpu.SemaphoreType.DMA(())],
        compiler_params=_params(("arbitrary", "arbitrary", "arbitrary")),
        name="mlp_down",
    )(a, w, resid, w_gate, g_gate)


def _row_scale_kernel(ssq_ref, r_ref):
    s = ssq_ref[:, 0:LANES]
    for j in range(1, ssq_ref.shape[1] // LANES):
        s = s + ssq_ref[:, j * LANES:(j + 1) * LANES]
    r_ref[...] = lax.rsqrt(s * (1.0 / D_MODEL) + EPS)


def _row_scale(ssq, name, tm=1024):
    m, w = ssq.shape
    return pl.pallas_call(
        _row_scale_kernel,
        grid=(m // tm,),
        in_specs=[pl.BlockSpec((tm, w), lambda i: (i, 0))],
        out_specs=pl.BlockSpec((tm, LANES), lambda i: (i, 0)),
        out_shape=jax.ShapeDtypeStruct((m, LANES), F32),
        compiler_params=_params(("parallel",)),
        name=name,
    )(ssq)


def _pool_kernel(u_ref, w_ref, scale_ref, o_ref, ext_ref):
    ts = u_ref.shape[0]
    i = pl.program_id(0)

    @pl.when(i == 0)
    def _():
        ext_ref[0:POOL_HALO, :] = jnp.zeros((POOL_HALO, D_POOL), F32)

    @pl.when(i != 0)
    def _():
        ext_ref[0:POOL_HALO, :] = ext_ref[ts:ts + POOL_HALO, :]

    ext_ref[POOL_HALO:POOL_HALO + ts, :] = u_ref[...]
    t = i * ts + lax.broadcasted_iota(jnp.int32, (ts, 1), 0)
    for g, win in enumerate(POOL_WINDOWS):
        cols = slice(g * POOL_GROUP, (g + 1) * POOL_GROUP)
        u = ext_ref[POOL_HALO:POOL_HALO + ts, cols]
        win_sum = u
        for j in range(1, win):
            win_sum = win_sum + ext_ref[POOL_HALO - j:POOL_HALO - j + ts, cols]
        cnt = jnp.minimum(t + 1, win).astype(F32)
        z = (win_sum / cnt - u).astype(BF16)
        y = jnp.dot(z, w_ref[g], preferred_element_type=F32)
        o_ref[:, cols] = (y * scale_ref[:, cols]).astype(o_ref.dtype)


def _pool(proj, w_pool, scale, ts=512):
    s = proj.shape[0]
    return pl.pallas_call(
        _pool_kernel,
        grid=(s // ts,),
        in_specs=[
            pl.BlockSpec((ts, D_POOL), lambda i: (i, 0)),
            pl.BlockSpec((len(POOL_WINDOWS), POOL_GROUP, POOL_GROUP), lambda i: (0, 0, 0)),
            pl.BlockSpec((1, D_POOL), lambda i: (0, 0)),
        ],
        out_specs=pl.BlockSpec((ts, D_POOL), lambda i: (i, 0)),
        out_shape=jax.ShapeDtypeStruct((s, D_POOL), BF16),
        scratch_shapes=[pltpu.VMEM((ts + POOL_HALO, D_POOL), F32)],
        compiler_params=_params(("arbitrary",)),
        name="pool_mixer",
    )(proj, w_pool, scale)


def _mlstm_kernel(qk_ref, v_ref, o_ref, li_ref, b_ref, mx_ref, lirow_ref, brow_ref,
                  cw_ref, cb_ref, gh_ref, y_ref, ext_ref, c_ref, n_ref, m_ref):
    L = CHUNK
    i = pl.program_id(0)

    @pl.when(i == 0)
    def _():
        ext_ref[0:CONV_HALO, :] = jnp.zeros((CONV_HALO, D_QK), F32)
        c_ref[...] = jnp.zeros_like(c_ref)
        n_ref[...] = jnp.zeros_like(n_ref)
        m_ref[...] = jnp.zeros_like(m_ref)

    @pl.when(i != 0)
    def _():
        ext_ref[0:CONV_HALO, :] = ext_ref[L:L + CONV_HALO, :]

    ext_ref[CONV_HALO:CONV_HALO + L, :] = qk_ref[...]

    t_idx = lax.broadcasted_iota(jnp.int32, (L, L), 0)
    s_idx = lax.broadcasted_iota(jnp.int32, (L, L), 1)
    causal = s_idx <= t_idx

    def conv_silu(cols):
        y = cb_ref[:, cols]
        for j in range(CONV_W):
            off = CONV_HALO - (CONV_W - 1) + j
            y = y + cw_ref[j:j + 1, cols] * ext_ref[off:off + L, cols]
        return y * jax.nn.sigmoid(y)

    for h in range(N_HEADS):
        q = conv_silu(slice(h * DQK, (h + 1) * DQK))
        k = conv_silu(slice(N_HEADS * DQK + h * DQK, N_HEADS * DQK + (h + 1) * DQK)) * (DQK ** -0.5)
        qb = q.astype(BF16)
        kb = k.astype(BF16)
        vcols = slice(h * DV, (h + 1) * DV)
        v = v_ref[:, vcols].astype(BF16)
        li_col = li_ref[:, h:h + 1]
        b_col = b_ref[:, h:h + 1]
        mx_col = mx_ref[:, h:h + 1]
        li_row = lirow_ref[0, h:h + 1, :]
        b_row = brow_ref[0, h:h + 1, :]
        m_prev = m_ref[h][:, 0:1]
        c_prev = c_ref[h]
        n_prev = n_ref[h]

        d_mat = jnp.where(causal, b_col - b_row + li_row, -jnp.inf)
        e_col = b_col + m_prev
        m_t = jnp.maximum(mx_col, e_col)
        s_qk = lax.dot_general(qb, kb, (((1,), (1,)), ((), ())), preferred_element_type=F32)
        w_mat = jnp.exp(d_mat - m_t) * s_qk
        w_inter = jnp.exp(e_col - m_t)
        num = jnp.dot(w_mat.astype(BF16), v, preferred_element_type=F32)
        num = num + w_inter * jnp.dot(qb, c_prev.astype(BF16), preferred_element_type=F32)
        den = jnp.sum(w_mat, axis=-1, keepdims=True)
        den = den + w_inter * jnp.sum(q * n_prev, axis=-1, keepdims=True)
        inv = 1.0 / jnp.maximum(jnp.abs(den), jnp.exp(-m_t))
        msq = jnp.mean(num * num, axis=-1, keepdims=True)
        scale = inv * lax.rsqrt(inv * inv * msq + EPS)
        hh = num * scale * (gh_ref[:, vcols] * jax.nn.sigmoid(o_ref[:, vcols]))
        y_ref[:, vcols] = hh.astype(y_ref.dtype)

        g_tot = b_col[L - 1:L, :]
        a_col = g_tot + li_col - b_col
        m_loc = jnp.max(a_col, axis=0, keepdims=True)
        kw = k * jnp.exp(a_col - m_loc)
        c_loc = lax.dot_general(kw.astype(BF16), v, (((0,), (0,)), ((), ())), preferred_element_type=F32)
        n_loc = jnp.sum(kw, axis=0, keepdims=True)
        m_new = jnp.maximum(g_tot + m_prev, m_loc)
        s_old = jnp.exp(g_tot + m_prev - m_new)
        s_loc = jnp.exp(m_loc - m_new)
        c_ref[h] = s_old * c_prev + s_loc * c_loc
        n_ref[h] = s_old * n_prev + s_loc * n_loc
        m_ref[h] = jnp.broadcast_to(m_new, (1, LANES))


def _mlstm(proj, li, b, mx, lirow, brow, conv_w, conv_b, g_head):
    s = proj.shape[0]
    L = CHUNK
    col_spec = pl.BlockSpec((L, LANES), lambda i: (i, 0))
    row_spec = pl.BlockSpec((1, N_HEADS, L), lambda i: (i, 0, 0))
    return pl.pallas_call(
        _mlstm_kernel,
        grid=(s // L,),
        in_specs=[
            pl.BlockSpec((L, D_QK), lambda i: (i, D_POOL // D_QK)),
            pl.BlockSpec((L, D_MLSTM), lambda i: (i, (D_POOL + D_QK) // D_MLSTM)),
            pl.BlockSpec((L, D_MLSTM), lambda i: (i, (D_POOL + D_QK + D_MLSTM) // D_MLSTM)),
            col_spec, col_spec, col_spec, row_spec, row_spec,
            pl.BlockSpec((CONV_W, D_QK), lambda i: (0, 0)),
            pl.BlockSpec((1, D_QK), lambda i: (0, 0)),
            pl.BlockSpec((1, D_MLSTM), lambda i: (0, 0)),
        ],
        out_specs=pl.BlockSpec((L, D_MLSTM), lambda i: (i, 0)),
        out_shape=jax.ShapeDtypeStruct((s, D_MLSTM), BF16),
        scratch_shapes=[
            pltpu.VMEM((L + CONV_HALO, D_QK), F32),
            pltpu.VMEM((N_HEADS, DQK, DV), F32),
            pltpu.VMEM((N_HEADS, 1, DQK), F32),
            pltpu.VMEM((N_HEADS, 1, LANES), F32),
        ],
        compiler_params=_params(("arbitrary",)),
        name="mlstm_mixer",
    )(proj, proj, proj, li, b, mx, lirow, brow, conv_w, conv_b, g_head)


def kernel(x, p, g_mix, w_in, conv_w, conv_b, b_igate, b_fgate, g_head, w_pool, pool_scale,
           w_out, g_mlp, w_up, w_down, g_ple, w_ple_gate, b_ple_gate, w_ple, g_final):
    depth = w_in.shape[0]
    h = x.reshape(SEQ, D_MODEL)
    for l in range(depth):
        w_g = jnp.pad(w_in[l, :, D_MAIN:], ((0, 0), (0, LANES - 2 * N_HEADS))).astype(BF16)
        gate_bias = jnp.pad(jnp.concatenate([b_igate[l], b_fgate[l]]), (0, LANES - 2 * N_HEADS)).reshape(1, LANES)

        hn, li, b, mx, lirow, brow = _norm_gates(h, g_mix[l].reshape(1, D_MODEL), w_g, gate_bias.astype(F32))
        proj = _proj_in(l, hn, jnp.swapaxes(w_in, 1, 2))
        y_pool = _pool(proj, w_pool[l].astype(BF16), pool_scale[l].reshape(1, D_POOL))
        y_mlstm = _mlstm(proj, li, b, mx, lirow, brow, conv_w[l], conv_b[l].reshape(1, D_QK),
                         g_head[l].reshape(1, D_MLSTM))
        h, hb, ssq = _mix_out(l, y_pool, y_mlstm, w_out, h)

        r = _row_scale(ssq, "scale_mlp")
        act, w_down_b = _mlp_up(l, hb, w_up, g_mlp[l].reshape(D_MODEL, 1), r, w_down)
        h, hb, ssq, w_gate_b = _down(l, act, w_down_b, h, w_ple_gate, g_ple[l].reshape(D_MODEL, 1))

        assert depth == 1
        h = _ple_final(hb, w_gate_b, ssq, b_ple_gate[l].reshape(1, D_MODEL), p[l].reshape(SEQ, D_PLE).astype(BF16),
                       w_ple[l].astype(BF16), h, g_final.reshape(1, D_MODEL))
    return h.reshape(x.shape)
```

```python
import jax
import jax.numpy as jnp
from jax import lax
from jax.experimental import pallas as pl
from jax.experimental.pallas import tpu as pltpu

F32 = jnp.float32
BF16 = jnp.bfloat16

D_MODEL = 4096
SEQ = 8192
D_POOL = 2048
POOL_WINDOWS = (2, 4, 8, 16)
POOL_GROUP = 512
SUBLANES = 8
POOL_HALO = 32
D_MLSTM = 2048
N_HEADS = 8
DV = 256
DQK = 128
D_QK = 2 * N_HEADS * DQK
CONV_W = 4
CONV_HALO = 8
GATE_CAP = 15.0
D_FF = 4 * D_MODEL
D_PLE = 256
EPS = 1e-6
D_MAIN = D_POOL + D_QK + 2 * D_MLSTM
LANES = 128
CHUNK = 256
VMEM_LIMIT = 58 * 1024 * 1024


def _params(sem, vmem=VMEM_LIMIT):
    return pltpu.CompilerParams(dimension_semantics=sem, vmem_limit_bytes=vmem)


def _chunk_scan(x, pos, combine, fill):
    sh = 1
    while sh < CHUNK:
        prev = jnp.where(pos >= sh, pltpu.roll(x, sh, 0), fill)
        x = combine(x, prev)
        sh *= 2
    return x


def _norm_gates_kernel(x_ref, g_ref, w_ref, bias_ref, hn_ref, li_ref, b_ref, mx_ref, lirow_ref, brow_ref):
    tm = x_ref.shape[0]
    x = x_ref[...]
    ms = jnp.mean(x * x, axis=-1, keepdims=True)
    hn = (x * lax.rsqrt(ms + EPS) * g_ref[...]).astype(BF16)
    hn_ref[...] = hn
    pre = jnp.dot(hn, w_ref[...], preferred_element_type=F32) + bias_ref[...]
    pre = GATE_CAP * jnp.tanh(pre / GATE_CAP)
    logf = jnp.minimum(pre, 0.0) - jnp.log1p(jnp.exp(-jnp.abs(pre)))
    pos = lax.broadcasted_iota(jnp.int32, (tm, LANES), 0) % CHUNK
    b = _chunk_scan(logf, pos, jnp.add, 0.0)
    b = pltpu.roll(b, LANES - N_HEADS, 1)
    li = pre
    mx = b + _chunk_scan(li - b, pos, jnp.maximum, -jnp.inf)
    li_ref[...] = li
    b_ref[...] = b
    mx_ref[...] = mx
    for c in range(tm // CHUNK):
        rows = slice(c * CHUNK, (c + 1) * CHUNK)
        lirow_ref[c] = li[rows].T[:N_HEADS]
        brow_ref[c] = b[rows].T[:N_HEADS]


def _norm_gates(x, g, w_g, bias, tm=512):
    m, d = x.shape
    col = jax.ShapeDtypeStruct((m, LANES), F32)
    row = jax.ShapeDtypeStruct((m // CHUNK, N_HEADS, CHUNK), F32)
    col_spec = pl.BlockSpec((tm, LANES), lambda i: (i, 0))
    row_spec = pl.BlockSpec((tm // CHUNK, N_HEADS, CHUNK), lambda i: (i, 0, 0))
    return pl.pallas_call(
        _norm_gates_kernel,
        grid=(m // tm,),
        in_specs=[
            pl.BlockSpec((tm, d), lambda i: (i, 0)),
            pl.BlockSpec((1, d), lambda i: (0, 0)),
            pl.BlockSpec((d, LANES), lambda i: (0, 0)),
            pl.BlockSpec((1, LANES), lambda i: (0, 0)),
        ],
        out_specs=[pl.BlockSpec((tm, d), lambda i: (i, 0)), col_spec, col_spec, col_spec, row_spec, row_spec],
        out_shape=[jax.ShapeDtypeStruct((m, d), BF16), col, col, col, row, row],
        compiler_params=_params(("parallel",)),
        name="norm_gates",
    )(x, g, w_g, bias)


def _ws_maps(nb, n_i):
    row = lambda b, i: jnp.where(b > 0, i, 0)
    blk = lambda b: jnp.maximum(b - 1, 0)
    wrow = lambda b, i: jnp.where(b < nb, i, n_i - 1)
    wblk = lambda b: jnp.minimum(b, nb - 1)
    return row, blk, wrow, wblk


def _ws_step(wchunk_ref, g_ref, wbufs, compute):
    b = pl.program_id(0)
    i = pl.program_id(1)

    def stage(buf):
        w = wchunk_ref[...]
        if g_ref is not None:
            w = w * g_ref[...]
        ck = w.shape[0]
        buf[pl.ds(pl.multiple_of(i * ck, ck), ck), :] = w.astype(BF16)

    @pl.when(b == 0)
    def _():
        stage(wbufs[0])

    for parity in (0, 1):
        @pl.when(jnp.logical_and(b > 0, b % 2 == parity))
        def _():
            stage(wbufs[parity])
            compute(wbufs[1 - parity])


def _row_sumsq(h):
    return jnp.broadcast_to(jnp.sum(h * h, axis=1, keepdims=True), (h.shape[0], LANES))


def _lane_tile(r, n):
    return jnp.concatenate([r] * (n // LANES), axis=1)


def _sigmoid(x):
    return 0.5 * jnp.tanh(0.5 * x) + 0.5


def _proj_kernel(a_ref, wchunk_ref, o_ref, wbuf0, wbuf1):
    b = pl.program_id(0)
    i = pl.program_id(1)
    wbufs = (wbuf0, wbuf1)

    def stage(buf):
        w = wchunk_ref[...].T
        ck = w.shape[0]
        buf[pl.ds(pl.multiple_of(i * ck, ck), ck), :] = w.astype(BF16)

    @pl.when(b == 0)
    def _():
        stage(wbuf0)

    for parity in (0, 1):
        @pl.when(jnp.logical_and(b > 0, b % 2 == parity))
        def _():
            stage(wbufs[parity])
            o_ref[...] = jnp.dot(a_ref[...], wbufs[1 - parity][...], preferred_element_type=F32)

def _mix_out_kernel(yp_ref, ym_ref, wchunk_ref, res_ref, o_ref, ob_ref, ssq_ref, wbuf0, wbuf1):
    def compute(w_ref):
        acc = jnp.dot(yp_ref[...], w_ref[0:D_POOL, :], preferred_element_type=F32)
        acc += jnp.dot(ym_ref[...], w_ref[D_POOL:D_POOL + D_MLSTM, :], preferred_element_type=F32)
        h = res_ref[...] + acc
        o_ref[...] = h
        ob_ref[...] = h.astype(BF16)
        ssq_ref[...] = _row_sumsq(h)
    _ws_step(wchunk_ref, None, (wbuf0, wbuf1), compute)


def _up_kernel(a_ref, wchunk_ref, g_ref, r_ref, wd_ref, o_ref, wdb_ref, wbuf0, wbuf1):
    def compute(w_ref):
        y = jnp.dot(a_ref[...], w_ref[...], preferred_element_type=F32)
        y = jnp.maximum(y * _lane_tile(r_ref[...], y.shape[1]), 0.0)
        o_ref[...] = (y * y).astype(BF16)
        wdb_ref[...] = wd_ref[...].astype(BF16)
    _ws_step(wchunk_ref, g_ref, (wbuf0, wbuf1), compute)


def _ws_call(kern, name, l, a_list, w, k_dim, n_out, tm, tn, extra_in, extra_specs, out_shapes, out_specs):
    m = a_list[0].shape[0]
    nb, n_i = n_out // tn, m // tm
    ck = k_dim // n_i
    row, blk, wrow, wblk = _ws_maps(nb, n_i)
    a_specs = [pl.BlockSpec((tm, a.shape[1]), lambda b, i: (row(b, i), 0)) for a in a_list]
    w_spec = pl.BlockSpec((None, ck, tn), lambda b, i: (l, wrow(b, i), wblk(b)))
    return pl.pallas_call(
        kern,
        grid=(nb + 1, n_i),
        in_specs=a_specs + [w_spec] + extra_specs(row, blk, wrow, wblk),
        out_specs=out_specs(row, blk),
        out_shape=out_shapes,
        scratch_shapes=[pltpu.VMEM((k_dim, tn), BF16), pltpu.VMEM((k_dim, tn), BF16)],
        compiler_params=_params(("arbitrary", "arbitrary")),
        name=name,
    )(*a_list, w, *extra_in)


def _proj_in(l, hn, w_in_t, tm=1024, tn=1024):
    m = hn.shape[0]
    nb, n_i = D_MAIN // tn, m // tm
    ck = D_MODEL // n_i
    row, blk, wrow, wblk = _ws_maps(nb, n_i)
    return pl.pallas_call(
        _proj_kernel,
        grid=(nb + 1, n_i),
        in_specs=[
            pl.BlockSpec((tm, D_MODEL), lambda b, i: (row(b, i), 0)),
            pl.BlockSpec((None, tn, ck), lambda b, i: (l, wblk(b), wrow(b, i))),
        ],
        out_specs=pl.BlockSpec((tm, tn), lambda b, i: (row(b, i), blk(b))),
        out_shape=jax.ShapeDtypeStruct((m, D_MAIN), F32),
        scratch_shapes=[pltpu.VMEM((D_MODEL, tn), BF16), pltpu.VMEM((D_MODEL, tn), BF16)],
        compiler_params=_params(("arbitrary", "arbitrary")),
        name="proj_in",
    )(hn, w_in_t)


def _mix_out(l, y_pool, y_mlstm, w_out, resid, tm=512, tn=1024):
    m = resid.shape[0]
    nb = D_MODEL // tn
    tile = lambda row, blk: pl.BlockSpec((tm, tn), lambda b, i: (row(b, i), blk(b)))
    return _ws_call(
        _mix_out_kernel, "mix_out", l, [y_pool, y_mlstm], w_out, D_MODEL, D_MODEL, tm, tn, [resid],
        lambda row, blk, wrow, wblk: [tile(row, blk)],
        [jax.ShapeDtypeStruct((m, D_MODEL), F32), jax.ShapeDtypeStruct((m, D_MODEL), BF16),
         jax.ShapeDtypeStruct((m, nb * LANES), F32)],
        lambda row, blk: [tile(row, blk), tile(row, blk),
                          pl.BlockSpec((tm, LANES), lambda b, i: (row(b, i), blk(b)))])


def _mlp_up(l, hb, w_up, g, r, w_down, tm=1024, tn=1024):
    m = hb.shape[0]
    nb, n_i = D_FF // tn, m // tm
    ck = D_MODEL // n_i
    wd_rows = D_FF // (nb * n_i)
    side = lambda row, blk: (lambda b, i: (blk(b) * n_i + row(b, i), 0))
    return _ws_call(
        _up_kernel, "mlp_up", l, [hb], w_up, D_MODEL, D_FF, tm, tn, [g, r, w_down],
        lambda row, blk, wrow, wblk: [
            pl.BlockSpec((ck, 1), lambda b, i: (wrow(b, i), 0)),
            pl.BlockSpec((tm, LANES), lambda b, i: (row(b, i), 0)),
            pl.BlockSpec((None, wd_rows, D_MODEL), lambda b, i: (l,) + side(row, blk)(b, i)),
        ],
        [jax.ShapeDtypeStruct((m, D_FF), BF16), jax.ShapeDtypeStruct((D_FF, D_MODEL), BF16)],
        lambda row, blk: [pl.BlockSpec((tm, tn), lambda b, i: (row(b, i), blk(b))),
                          pl.BlockSpec((wd_rows, D_MODEL), side(row, blk))])


def _ple_kernel(a_ref, w_ref, ssq_ref, bias_ref, p_ref, wp_ref, res_ref, gf_ref, o_ref):
    j = pl.program_id(1)
    tn = w_ref.shape[1]
    n_j = o_ref.shape[1] // tn

    def tile():
        s = ssq_ref[:, 0:LANES]
        for c in range(1, ssq_ref.shape[1] // LANES):
            s = s + ssq_ref[:, c * LANES:(c + 1) * LANES]
        r = lax.rsqrt(s * (1.0 / D_MODEL) + EPS)
        z = jnp.dot(a_ref[...], w_ref[...], preferred_element_type=F32)
        z = z * _lane_tile(r, tn) + bias_ref[...]
        pe = jnp.dot(p_ref[...], wp_ref[...], preferred_element_type=F32)
        return res_ref[...] + jax.nn.sigmoid(z) * pe

    for jj in range(n_j):
        @pl.when(j == jj)
        def _():
            o_ref[:, jj * tn:(jj + 1) * tn] = tile()
            if jj == n_j - 1:
                x = o_ref[...]
                ms = jnp.mean(x * x, axis=-1, keepdims=True)
                o_ref[...] = x * lax.rsqrt(ms + EPS) * gf_ref[...]


def _ple_final(hb, w_gate_b, ssq, b_gate, p, w_ple, resid, g_final, tm=512, tn=1024):
    m = hb.shape[0]
    return pl.pallas_call(
        _ple_kernel,
        grid=(m // tm, D_MODEL // tn),
        in_specs=[
            pl.BlockSpec((tm, D_MODEL), lambda i, j: (i, 0)),
            pl.BlockSpec((D_MODEL, tn), lambda i, j: (0, j)),
            pl.BlockSpec((tm, ssq.shape[1]), lambda i, j: (i, 0)),
            pl.BlockSpec((1, tn), lambda i, j: (0, j)),
            pl.BlockSpec((tm, D_PLE), lambda i, j: (i, 0)),
            pl.BlockSpec((D_PLE, tn), lambda i, j: (0, j)),
            pl.BlockSpec((tm, tn), lambda i, j: (i, j)),
            pl.BlockSpec((1, D_MODEL), lambda i, j: (0, 0)),
        ],
        out_specs=pl.BlockSpec((tm, D_MODEL), lambda i, j: (i, 0)),
        out_shape=jax.ShapeDtypeStruct((m, D_MODEL), F32),
        compiler_params=_params(("parallel", "arbitrary")),
        name="ple_final",
    )(hb, w_gate_b, ssq, b_gate, p, w_ple, resid, g_final)


def _down_kernel(a_ref, w_ref, res_hbm, wg_ref, g_ref, o_ref, ob_ref, ssq_ref, wgb_ref, res_sem):
    i, j, k = pl.program_id(0), pl.program_id(1), pl.program_id(2)
    last = pl.num_programs(2) - 1
    tm, tn = o_ref.shape
    wgb_ref[...] = (wg_ref[...] * g_ref[...]).astype(BF16)

    def part():
        return jnp.dot(a_ref[...], w_ref[...], preferred_element_type=F32)

    @pl.when(k == 0)
    def _():
        rows = pl.ds(pl.multiple_of(i * tm, tm), tm)
        cols = pl.ds(pl.multiple_of(j * tn, tn), tn)
        res_copy = pltpu.make_async_copy(res_hbm.at[rows, cols], o_ref, res_sem)
        res_copy.start()
        p = part()
        res_copy.wait()
        o_ref[...] += p

    @pl.when(jnp.logical_and(k != 0, k != last))
    def _():
        o_ref[...] += part()

    @pl.when(k == last)
    def _():
        h = o_ref[...] + part()
        o_ref[...] = h
        ob_ref[...] = h.astype(BF16)
        ssq_ref[...] = _row_sumsq(h)


def _down(l, a, w, resid, w_gate, g_gate, tm=1024, tn=1024, tk=4096):
    m, k = a.shape
    n = w.shape[1]
    n_j, n_k = n // tn, k // tk
    side_rows = w_gate.shape[1] // ((m // tm) * n_j * n_k)
    tile = pl.BlockSpec((tm, tn), lambda i, j, kk: (i, j))
    step = lambda i, j, kk: (i * n_j + j) * n_k + kk
    return pl.pallas_call(
        _down_kernel,
        grid=(m // tm, n_j, n_k),
        in_specs=[
            pl.BlockSpec((tm, tk), lambda i, j, kk: (i, kk)),
            pl.BlockSpec((tk, tn), lambda i, j, kk: (kk, j)),
            pl.BlockSpec(memory_space=pl.ANY),
            pl.BlockSpec((None, side_rows, w_gate.shape[2]), lambda i, j, kk: (l, step(i, j, kk), 0)),
            pl.BlockSpec((side_rows, 1), lambda i, j, kk: (step(i, j, kk), 0)),
        ],
        out_specs=[tile, tile, pl.BlockSpec((tm, LANES), lambda i, j, kk: (i, j)),
                   pl.BlockSpec((side_rows, w_gate.shape[2]), lambda i, j, kk: (step(i, j, kk), 0))],
        out_shape=[jax.ShapeDtypeStruct((m, n), F32), jax.ShapeDtypeStruct((m, n), BF16),
                   jax.ShapeDtypeStruct((m, n_j * LANES), F32),
                   jax.ShapeDtypeStruct(w_gate.shape[1:], BF16)],
        scratch_shapes=[pltpu.SemaphoreType.DMA(())],
        compiler_params=_params(("arbitrary", "arbitrary", "arbitrary")),
        name="mlp_down",
    )(a, w, resid, w_gate, g_gate)


def _row_scale_kernel(ssq_ref, r_ref):
    s = ssq_ref[:, 0:LANES]
    for j in range(1, ssq_ref.shape[1] // LANES):
        s = s + ssq_ref[:, j * LANES:(j + 1) * LANES]
    r_ref[...] = lax.rsqrt(s * (1.0 / D_MODEL) + EPS)


def _row_scale(ssq, name, tm=1024):
    m, w = ssq.shape
    return pl.pallas_call(
        _row_scale_kernel,
        grid=(m // tm,),
        in_specs=[pl.BlockSpec((tm, w), lambda i: (i, 0))],
        out_specs=pl.BlockSpec((tm, LANES), lambda i: (i, 0)),
        out_shape=jax.ShapeDtypeStruct((m, LANES), F32),
        compiler_params=_params(("parallel",)),
        name=name,
    )(ssq)


def _pool_kernel(u_ref, w_ref, scale_ref, o_ref, ext_ref, sum_a_ref, sum_b_ref):
    ts = u_ref.shape[0]
    rows = POOL_HALO + ts
    i = pl.program_id(0)

    @pl.when(i == 0)
    def _():
        ext_ref[0:POOL_HALO, :] = jnp.zeros((POOL_HALO, D_POOL), F32)

    @pl.when(i != 0)
    def _():
        ext_ref[0:POOL_HALO, :] = ext_ref[ts:rows, :]

    ext_ref[POOL_HALO:rows, :] = u_ref[...]
    t = i * ts + lax.broadcasted_iota(jnp.int32, (ts, 1), 0)
    for g, win in enumerate(POOL_WINDOWS):
        cols = slice(g * POOL_GROUP, (g + 1) * POOL_GROUP)
        u = ext_ref[POOL_HALO:rows, cols]
        src, src_cols, lo, shift, stage = ext_ref, cols, 0, 1, 0
        while True:
            lo += SUBLANES
            sums = src[lo:rows, src_cols] + src[lo - shift:rows - shift, src_cols]
            shift *= 2
            if shift == win:
                break
            dst = (sum_a_ref, sum_b_ref)[stage % 2]
            dst[lo:rows, :] = sums
            src, src_cols, stage = dst, slice(None), stage + 1
        win_sum = sums[POOL_HALO - lo:, :]
        cnt = jnp.minimum(t + 1, win).astype(F32)
        z = (win_sum / cnt - u).astype(BF16)
        y = jnp.dot(z, w_ref[g], preferred_element_type=F32)
        o_ref[:, cols] = (y * scale_ref[:, cols]).astype(o_ref.dtype)


def _pool(proj, w_pool, scale, ts=512):
    s = proj.shape[0]
    return pl.pallas_call(
        _pool_kernel,
        grid=(s // ts,),
        in_specs=[
            pl.BlockSpec((ts, D_POOL), lambda i: (i, 0)),
            pl.BlockSpec((len(POOL_WINDOWS), POOL_GROUP, POOL_GROUP), lambda i: (0, 0, 0)),
            pl.BlockSpec((1, D_POOL), lambda i: (0, 0)),
        ],
        out_specs=pl.BlockSpec((ts, D_POOL), lambda i: (i, 0)),
        out_shape=jax.ShapeDtypeStruct((s, D_POOL), BF16),
        scratch_shapes=[pltpu.VMEM((ts + POOL_HALO, D_POOL), F32),
                        pltpu.VMEM((ts + POOL_HALO, POOL_GROUP), F32), pltpu.VMEM((ts + POOL_HALO, POOL_GROUP), F32)],
        compiler_params=_params(("arbitrary",)),
        name="pool_mixer",
    )(proj, w_pool, scale)


def _mlstm_kernel(qk_ref, v_ref, o_ref, li_ref, b_ref, mx_ref, lirow_ref, brow_ref,
                  cw_ref, cb_ref, gh_ref, y_ref, ext_ref, c_ref, n_ref, m_ref):
    L = CHUNK
    i = pl.program_id(0)

    @pl.when(i == 0)
    def _():
        ext_ref[0:CONV_HALO, :] = jnp.zeros((CONV_HALO, D_QK), F32)
        c_ref[...] = jnp.zeros_like(c_ref)
        n_ref[...] = jnp.zeros_like(n_ref)
        m_ref[...] = jnp.zeros_like(m_ref)

    @pl.when(i != 0)
    def _():
        ext_ref[0:CONV_HALO, :] = ext_ref[L:L + CONV_HALO, :]

    ext_ref[CONV_HALO:CONV_HALO + L, :] = qk_ref[...]

    t_idx = lax.broadcasted_iota(jnp.int32, (L, L), 0)
    s_idx = lax.broadcasted_iota(jnp.int32, (L, L), 1)
    causal = s_idx <= t_idx

    def conv_silu(cols):
        y = cb_ref[:, cols]
        for j in range(CONV_W):
            off = CONV_HALO - (CONV_W - 1) + j
            y = y + cw_ref[j:j + 1, cols] * ext_ref[off:off + L, cols]
        return y * _sigmoid(y)

    for h in range(N_HEADS):
        q = conv_silu(slice(h * DQK, (h + 1) * DQK))
        k = conv_silu(slice(N_HEADS * DQK + h * DQK, N_HEADS * DQK + (h + 1) * DQK)) * (DQK ** -0.5)
        qb = q.astype(BF16)
        kb = k.astype(BF16)
        vcols = slice(h * DV, (h + 1) * DV)
        v = v_ref[:, vcols].astype(BF16)
        li_col = li_ref[:, h:h + 1]
        b_col = b_ref[:, h:h + 1]
        mx_col = mx_ref[:, h:h + 1]
        li_row = lirow_ref[0, h:h + 1, :]
        b_row = brow_ref[0, h:h + 1, :]
        m_prev = m_ref[h][:, 0:1]
        c_prev = c_ref[h]
        n_prev = n_ref[h]

        d_mat = jnp.where(causal, b_col - b_row + li_row, -jnp.inf)
        e_col = b_col + m_prev
        m_t = jnp.maximum(mx_col, e_col)
        s_qk = lax.dot_general(qb, kb, (((1,), (1,)), ((), ())), preferred_element_type=F32)
        w_mat = jnp.exp(d_mat - m_t) * s_qk
        w_inter = jnp.exp(e_col - m_t)
        num = jnp.dot(w_mat.astype(BF16), v, preferred_element_type=F32)
        num = num + w_inter * jnp.dot(qb, c_prev.astype(BF16), preferred_element_type=F32)
        den = jnp.sum(w_mat, axis=-1, keepdims=True)
        den = den + w_inter * jnp.sum(q * n_prev, axis=-1, keepdims=True)
        inv = 1.0 / jnp.maximum(jnp.abs(den), jnp.exp(-m_t))
        msq = jnp.mean(num * num, axis=-1, keepdims=True)
        scale = inv * lax.rsqrt(inv * inv * msq + EPS)
        hh = num * scale * (gh_ref[:, vcols] * _sigmoid(o_ref[:, vcols]))
        y_ref[:, vcols] = hh.astype(y_ref.dtype)

        g_tot = b_col[L - 1:L, :]
        a_col = g_tot + li_col - b_col
        m_loc = jnp.max(a_col, axis=0, keepdims=True)
        kw = k * jnp.exp(a_col - m_loc)
        c_loc = lax.dot_general(kw.astype(BF16), v, (((0,), (0,)), ((), ())), preferred_element_type=F32)
        n_loc = jnp.sum(kw, axis=0, keepdims=True)
        m_new = jnp.maximum(g_tot + m_prev, m_loc)
        s_old = jnp.exp(g_tot + m_prev - m_new)
        s_loc = jnp.exp(m_loc - m_new)
        c_ref[h] = s_old * c_prev + s_loc * c_loc
        n_ref[h] = s_old * n_prev + s_loc * n_loc
        m_ref[h] = jnp.broadcast_to(m_new, (1, LANES))


def _mlstm(proj, li, b, mx, lirow, brow, conv_w, conv_b, g_head):
    s = proj.shape[0]
    L = CHUNK
    col_spec = pl.BlockSpec((L, LANES), lambda i: (i, 0))
    row_spec = pl.BlockSpec((1, N_HEADS, L), lambda i: (i, 0, 0))
    return pl.pallas_call(
        _mlstm_kernel,
        grid=(s // L,),
        in_specs=[
            pl.BlockSpec((L, D_QK), lambda i: (i, D_POOL // D_QK)),
            pl.BlockSpec((L, D_MLSTM), lambda i: (i, (D_POOL + D_QK) // D_MLSTM)),
            pl.BlockSpec((L, D_MLSTM), lambda i: (i, (D_POOL + D_QK + D_MLSTM) // D_MLSTM)),
            col_spec, col_spec, col_spec, row_spec, row_spec,
            pl.BlockSpec((CONV_W, D_QK), lambda i: (0, 0)),
            pl.BlockSpec((1, D_QK), lambda i: (0, 0)),
            pl.BlockSpec((1, D_MLSTM), lambda i: (0, 0)),
        ],
        out_specs=pl.BlockSpec((L, D_MLSTM), lambda i: (i, 0)),
        out_shape=jax.ShapeDtypeStruct((s, D_MLSTM), BF16),
        scratch_shapes=[
            pltpu.VMEM((L + CONV_HALO, D_QK), F32),
            pltpu.VMEM((N_HEADS, DQK, DV), F32),
            pltpu.VMEM((N_HEADS, 1, DQK), F32),
            pltpu.VMEM((N_HEADS, 1, LANES), F32),
        ],
        compiler_params=_params(("arbitrary",)),
        name="mlstm_mixer",
    )(proj, proj, proj, li, b, mx, lirow, brow, conv_w, conv_b, g_head)


def kernel(x, p, g_mix, w_in, conv_w, conv_b, b_igate, b_fgate, g_head, w_pool, pool_scale,
           w_out, g_mlp, w_up, w_down, g_ple, w_ple_gate, b_ple_gate, w_ple, g_final):
    depth = w_in.shape[0]
    h = x.reshape(SEQ, D_MODEL)
    for l in range(depth):
        w_g = jnp.pad(w_in[l, :, D_MAIN:], ((0, 0), (0, LANES - 2 * N_HEADS))).astype(BF16)
        gate_bias = jnp.pad(jnp.concatenate([b_igate[l], b_fgate[l]]), (0, LANES - 2 * N_HEADS)).reshape(1, LANES)

        hn, li, b, mx, lirow, brow = _norm_gates(h, g_mix[l].reshape(1, D_MODEL), w_g, gate_bias.astype(F32))
        proj = _proj_in(l, hn, jnp.swapaxes(w_in, 1, 2))
        y_pool = _pool(proj, w_pool[l].astype(BF16), pool_scale[l].reshape(1, D_POOL))
        y_mlstm = _mlstm(proj, li, b, mx, lirow, brow, conv_w[l], conv_b[l].reshape(1, D_QK),
                         g_head[l].reshape(1, D_MLSTM))
        h, hb, ssq = _mix_out(l, y_pool, y_mlstm, w_out, h)

        r = _row_scale(ssq, "scale_mlp")
        act, w_down_b = _mlp_up(l, hb, w_up, g_mlp[l].reshape(D_MODEL, 1), r, w_down)
        h, hb, ssq, w_gate_b = _down(l, act, w_down_b, h, w_ple_gate, g_ple[l].reshape(D_MODEL, 1))

        assert depth == 1
        h = _ple_final(hb, w_gate_b, ssq, b_ple_gate[l].reshape(1, D_MODEL), p[l].reshape(SEQ, D_PLE).astype(BF16),
                       w_ple[l].astype(BF16), h, g_final.reshape(1, D_MODEL))
    return h.reshape(x.shape)
```

```python
import jax
import jax.numpy as jnp
from jax import lax
from jax.experimental import pallas as pl
from jax.experimental.pallas import tpu as pltpu

F32 = jnp.float32
BF16 = jnp.bfloat16

D_MODEL = 4096
SEQ = 8192
D_POOL = 2048
POOL_WINDOWS = (2, 4, 8, 16)
POOL_GROUP = 512
SUBLANES = 8
POOL_HALO = 32
D_MLSTM = 2048
N_HEADS = 8
DV = 256
DQK = 128
D_QK = 2 * N_HEADS * DQK
CONV_W = 4
CONV_HALO = 8
GATE_CAP = 15.0
D_FF = 4 * D_MODEL
D_PLE = 256
EPS = 1e-6
D_MAIN = D_POOL + D_QK + 2 * D_MLSTM
LANES = 128
CHUNK = 256
VMEM_LIMIT = 58 * 1024 * 1024


def _params(sem, vmem=VMEM_LIMIT):
    return pltpu.CompilerParams(dimension_semantics=sem, vmem_limit_bytes=vmem)


def _chunk_scan(x, pos, combine, fill):
    sh = 1
    while sh < CHUNK:
        prev = jnp.where(pos >= sh, pltpu.roll(x, sh, 0), fill)
        x = combine(x, prev)
        sh *= 2
    return x


def _norm_gates_kernel(x_ref, g_ref, w_ref, bias_ref, hn_ref, li_ref, b_ref, mx_ref, lirow_ref, brow_ref):
    tm = x_ref.shape[0]
    x = x_ref[...]
    ms = jnp.mean(x * x, axis=-1, keepdims=True)
    hn = (x * lax.rsqrt(ms + EPS) * g_ref[...]).astype(BF16)
    hn_ref[...] = hn
    pre = jnp.dot(hn, w_ref[...], preferred_element_type=F32) + bias_ref[...]
    pre = GATE_CAP * jnp.tanh(pre / GATE_CAP)
    logf = jnp.minimum(pre, 0.0) - jnp.log1p(jnp.exp(-jnp.abs(pre)))
    pos = lax.broadcasted_iota(jnp.int32, (tm, LANES), 0) % CHUNK
    b = _chunk_scan(logf, pos, jnp.add, 0.0)
    b = pltpu.roll(b, LANES - N_HEADS, 1)
    li = pre
    mx = b + _chunk_scan(li - b, pos, jnp.maximum, -jnp.inf)
    li_ref[...] = li
    b_ref[...] = b
    mx_ref[...] = mx
    for c in range(tm // CHUNK):
        rows = slice(c * CHUNK, (c + 1) * CHUNK)
        lirow_ref[c] = li[rows].T[:N_HEADS]
        brow_ref[c] = b[rows].T[:N_HEADS]


def _norm_gates(x, g, w_g, bias, tm=512):
    m, d = x.shape
    col = jax.ShapeDtypeStruct((m, LANES), F32)
    row = jax.ShapeDtypeStruct((m // CHUNK, N_HEADS, CHUNK), F32)
    col_spec = pl.BlockSpec((tm, LANES), lambda i: (i, 0))
    row_spec = pl.BlockSpec((tm // CHUNK, N_HEADS, CHUNK), lambda i: (i, 0, 0))
    return pl.pallas_call(
        _norm_gates_kernel,
        grid=(m // tm,),
        in_specs=[
            pl.BlockSpec((tm, d), lambda i: (i, 0)),
            pl.BlockSpec((1, d), lambda i: (0, 0)),
            pl.BlockSpec((d, LANES), lambda i: (0, 0)),
            pl.BlockSpec((1, LANES), lambda i: (0, 0)),
        ],
        out_specs=[pl.BlockSpec((tm, d), lambda i: (i, 0)), col_spec, col_spec, col_spec, row_spec, row_spec],
        out_shape=[jax.ShapeDtypeStruct((m, d), BF16), col, col, col, row, row],
        compiler_params=_params(("parallel",)),
        name="norm_gates",
    )(x, g, w_g, bias)


def _ws_maps(nb, n_i):
    row = lambda b, i: jnp.where(b > 0, i, 0)
    blk = lambda b: jnp.maximum(b - 1, 0)
    wrow = lambda b, i: jnp.where(b < nb, i, n_i - 1)
    wblk = lambda b: jnp.minimum(b, nb - 1)
    return row, blk, wrow, wblk


def _ws_step(wchunk_ref, g_ref, wbufs, compute):
    b = pl.program_id(0)
    i = pl.program_id(1)

    def stage(buf):
        w = wchunk_ref[...]
        if g_ref is not None:
            w = w * g_ref[...]
        ck = w.shape[0]
        buf[pl.ds(pl.multiple_of(i * ck, ck), ck), :] = w.astype(BF16)

    @pl.when(b == 0)
    def _():
        stage(wbufs[0])

    for parity in (0, 1):
        @pl.when(jnp.logical_and(b > 0, b % 2 == parity))
        def _():
            stage(wbufs[parity])
            compute(wbufs[1 - parity])


def _row_sumsq(h):
    return jnp.broadcast_to(jnp.sum(h * h, axis=1, keepdims=True), (h.shape[0], LANES))


def _lane_tile(r, n):
    return jnp.concatenate([r] * (n // LANES), axis=1)


def _sigmoid(x):
    return 0.5 * jnp.tanh(0.5 * x) + 0.5


def _proj_kernel(a_ref, wchunk_ref, o_ref, wbuf0, wbuf1):
    b = pl.program_id(0)
    i = pl.program_id(1)
    wbufs = (wbuf0, wbuf1)

    def stage(buf):
        w = wchunk_ref[...].T
        ck = w.shape[0]
        buf[pl.ds(pl.multiple_of(i * ck, ck), ck), :] = w.astype(BF16)

    @pl.when(b == 0)
    def _():
        stage(wbuf0)

    for parity in (0, 1):
        @pl.when(jnp.logical_and(b > 0, b % 2 == parity))
        def _():
            stage(wbufs[parity])
            o_ref[...] = jnp.dot(a_ref[...], wbufs[1 - parity][...], preferred_element_type=F32)

def _mix_out_kernel(yp_ref, ym_ref, wchunk_ref, res_ref, o_ref, ob_ref, ssq_ref, wbuf0, wbuf1):
    def compute(w_ref):
        acc = jnp.dot(yp_ref[...], w_ref[0:D_POOL, :], preferred_element_type=F32)
        acc += jnp.dot(ym_ref[...], w_ref[D_POOL:D_POOL + D_MLSTM, :], preferred_element_type=F32)
        h = res_ref[...] + acc
        o_ref[...] = h
        ob_ref[...] = h.astype(BF16)
        ssq_ref[...] = _row_sumsq(h)
    _ws_step(wchunk_ref, None, (wbuf0, wbuf1), compute)


def _up_kernel(a_ref, wchunk_ref, g_ref, r_ref, wd_ref, o_ref, wdb_ref, wbuf0, wbuf1):
    def compute(w_ref):
        y = jnp.dot(a_ref[...], w_ref[...], preferred_element_type=F32)
        y = jnp.maximum(y * _lane_tile(r_ref[...], y.shape[1]), 0.0)
        o_ref[...] = (y * y).astype(BF16)
        wdb_ref[...] = wd_ref[...].astype(BF16)
    _ws_step(wchunk_ref, g_ref, (wbuf0, wbuf1), compute)


def _ws_call(kern, name, l, a_list, w, k_dim, n_out, tm, tn, extra_in, extra_specs, out_shapes, out_specs):
    m = a_list[0].shape[0]
    nb, n_i = n_out // tn, m // tm
    ck = k_dim // n_i
    row, blk, wrow, wblk = _ws_maps(nb, n_i)
    a_specs = [pl.BlockSpec((tm, a.shape[1]), lambda b, i: (row(b, i), 0)) for a in a_list]
    w_spec = pl.BlockSpec((None, ck, tn), lambda b, i: (l, wrow(b, i), wblk(b)))
    return pl.pallas_call(
        kern,
        grid=(nb + 1, n_i),
        in_specs=a_specs + [w_spec] + extra_specs(row, blk, wrow, wblk),
        out_specs=out_specs(row, blk),
        out_shape=out_shapes,
        scratch_shapes=[pltpu.VMEM((k_dim, tn), BF16), pltpu.VMEM((k_dim, tn), BF16)],
        compiler_params=_params(("arbitrary", "arbitrary")),
        name=name,
    )(*a_list, w, *extra_in)


def _proj_in(l, hn, w_in_t, tm=1024, tn=1024):
    m = hn.shape[0]
    nb, n_i = D_MAIN // tn, m // tm
    ck = D_MODEL // n_i
    row, blk, wrow, wblk = _ws_maps(nb, n_i)
    return pl.pallas_call(
        _proj_kernel,
        grid=(nb + 1, n_i),
        in_specs=[
            pl.BlockSpec((tm, D_MODEL), lambda b, i: (row(b, i), 0)),
            pl.BlockSpec((None, tn, ck), lambda b, i: (l, wblk(b), wrow(b, i))),
        ],
        out_specs=pl.BlockSpec((tm, tn), lambda b, i: (row(b, i), blk(b))),
        out_shape=jax.ShapeDtypeStruct((m, D_MAIN), F32),
        scratch_shapes=[pltpu.VMEM((D_MODEL, tn), BF16), pltpu.VMEM((D_MODEL, tn), BF16)],
        compiler_params=_params(("arbitrary", "arbitrary")),
        name="proj_in",
    )(hn, w_in_t)


def _mix_out(l, y_pool, y_mlstm, w_out, resid, tm=512, tn=1024):
    m = resid.shape[0]
    nb = D_MODEL // tn
    tile = lambda row, blk: pl.BlockSpec((tm, tn), lambda b, i: (row(b, i), blk(b)))
    return _ws_call(
        _mix_out_kernel, "mix_out", l, [y_pool, y_mlstm], w_out, D_MODEL, D_MODEL, tm, tn, [resid],
        lambda row, blk, wrow, wblk: [tile(row, blk)],
        [jax.ShapeDtypeStruct((m, D_MODEL), F32), jax.ShapeDtypeStruct((m, D_MODEL), BF16),
         jax.ShapeDtypeStruct((m, nb * LANES), F32)],
        lambda row, blk: [tile(row, blk), tile(row, blk),
                          pl.BlockSpec((tm, LANES), lambda b, i: (row(b, i), blk(b)))])


def _mlp_up(l, hb, w_up, g, r, w_down, tm=1024, tn=1024):
    m = hb.shape[0]
    nb, n_i = D_FF // tn, m // tm
    ck = D_MODEL // n_i
    wd_rows = D_FF // (nb * n_i)
    side = lambda row, blk: (lambda b, i: (blk(b) * n_i + row(b, i), 0))
    return _ws_call(
        _up_kernel, "mlp_up", l, [hb], w_up, D_MODEL, D_FF, tm, tn, [g, r, w_down],
        lambda row, blk, wrow, wblk: [
            pl.BlockSpec((ck, 1), lambda b, i: (wrow(b, i), 0)),
            pl.BlockSpec((tm, LANES), lambda b, i: (row(b, i), 0)),
            pl.BlockSpec((None, wd_rows, D_MODEL), lambda b, i: (l,) + side(row, blk)(b, i)),
        ],
        [jax.ShapeDtypeStruct((m, D_FF), BF16), jax.ShapeDtypeStruct((D_FF, D_MODEL), BF16)],
        lambda row, blk: [pl.BlockSpec((tm, tn), lambda b, i: (row(b, i), blk(b))),
                          pl.BlockSpec((wd_rows, D_MODEL), side(row, blk))])


def _ple_kernel(a_ref, w_ref, ssq_ref, bias_ref, p_ref, wp_ref, res_hbm, gf_ref, o_ref, res_sem):
    tm, d = o_ref.shape
    rows = pl.ds(pl.multiple_of(pl.program_id(0) * tm, tm), tm)
    res_copy = pltpu.make_async_copy(res_hbm.at[rows, :], o_ref, res_sem)
    res_copy.start()
    s = ssq_ref[:, 0:LANES]
    for c in range(1, ssq_ref.shape[1] // LANES):
        s = s + ssq_ref[:, c * LANES:(c + 1) * LANES]
    r = lax.rsqrt(s * (1.0 / D_MODEL) + EPS)
    z = jnp.dot(a_ref[...], w_ref[...], preferred_element_type=F32)
    z = z * _lane_tile(r, d) + bias_ref[...]
    update = jax.nn.sigmoid(z) * jnp.dot(p_ref[...], wp_ref[...], preferred_element_type=F32)
    res_copy.wait()
    x = o_ref[...] + update
    ms = jnp.mean(x * x, axis=-1, keepdims=True)
    o_ref[...] = x * lax.rsqrt(ms + EPS) * gf_ref[...]


def _ple_final(hb, w_gate_b, ssq, b_gate, p, w_ple, resid, g_final, tm=256):
    m = hb.shape[0]
    resident = pl.Buffered(1)
    return pl.pallas_call(
        _ple_kernel,
        grid=(m // tm,),
        in_specs=[
            pl.BlockSpec((tm, D_MODEL), lambda i: (i, 0)),
            pl.BlockSpec((D_MODEL, D_MODEL), lambda i: (0, 0), pipeline_mode=resident),
            pl.BlockSpec((tm, ssq.shape[1]), lambda i: (i, 0)),
            pl.BlockSpec((1, D_MODEL), lambda i: (0, 0)),
            pl.BlockSpec((tm, D_PLE), lambda i: (i, 0)),
            pl.BlockSpec((D_PLE, D_MODEL), lambda i: (0, 0), pipeline_mode=resident),
            pl.BlockSpec(memory_space=pl.ANY),
            pl.BlockSpec((1, D_MODEL), lambda i: (0, 0)),
        ],
        out_specs=pl.BlockSpec((tm, D_MODEL), lambda i: (i, 0)),
        out_shape=jax.ShapeDtypeStruct((m, D_MODEL), F32),
        scratch_shapes=[pltpu.SemaphoreType.DMA(())],
        compiler_params=_params(("arbitrary",)),
        name="ple_final",
    )(hb, w_gate_b, ssq, b_gate, p, w_ple, resid, g_final)


def _down_kernel(a_ref, w_ref, res_hbm, wg_ref, g_ref, o_ref, ob_ref, ssq_ref, wgb_ref, res_sem):
    i, j, k = pl.program_id(0), pl.program_id(1), pl.program_id(2)
    last = pl.num_programs(2) - 1
    tm, tn = o_ref.shape
    wgb_ref[...] = (wg_ref[...] * g_ref[...]).astype(BF16)

    def part():
        return jnp.dot(a_ref[...], w_ref[...], preferred_element_type=F32)

    @pl.when(k == 0)
    def _():
        rows = pl.ds(pl.multiple_of(i * tm, tm), tm)
        cols = pl.ds(pl.multiple_of(j * tn, tn), tn)
        res_copy = pltpu.make_async_copy(res_hbm.at[rows, cols], o_ref, res_sem)
        res_copy.start()
        p = part()
        res_copy.wait()
        o_ref[...] += p

    @pl.when(jnp.logical_and(k != 0, k != last))
    def _():
        o_ref[...] += part()

    @pl.when(k == last)
    def _():
        h = o_ref[...] + part()
        o_ref[...] = h
        ob_ref[...] = h.astype(BF16)
        ssq_ref[...] = _row_sumsq(h)


def _down(l, a, w, resid, w_gate, g_gate, tm=1024, tn=1024, tk=4096):
    m, k = a.shape
    n = w.shape[1]
    n_j, n_k = n // tn, k // tk
    side_rows = w_gate.shape[1] // ((m // tm) * n_j * n_k)
    tile = pl.BlockSpec((tm, tn), lambda i, j, kk: (i, j))
    step = lambda i, j, kk: (i * n_j + j) * n_k + kk
    return pl.pallas_call(
        _down_kernel,
        grid=(m // tm, n_j, n_k),
        in_specs=[
            pl.BlockSpec((tm, tk), lambda i, j, kk: (i, kk)),
            pl.BlockSpec((tk, tn), lambda i, j, kk: (kk, j)),
            pl.BlockSpec(memory_space=pl.ANY),
            pl.BlockSpec((None, side_rows, w_gate.shape[2]), lambda i, j, kk: (l, step(i, j, kk), 0)),
            pl.BlockSpec((side_rows, 1), lambda i, j, kk: (step(i, j, kk), 0)),
        ],
        out_specs=[tile, tile, pl.BlockSpec((tm, LANES), lambda i, j, kk: (i, j)),
                   pl.BlockSpec((side_rows, w_gate.shape[2]), lambda i, j, kk: (step(i, j, kk), 0))],
        out_shape=[jax.ShapeDtypeStruct((m, n), F32), jax.ShapeDtypeStruct((m, n), BF16),
                   jax.ShapeDtypeStruct((m, n_j * LANES), F32),
                   jax.ShapeDtypeStruct(w_gate.shape[1:], BF16)],
        scratch_shapes=[pltpu.SemaphoreType.DMA(())],
        compiler_params=_params(("arbitrary", "arbitrary", "arbitrary")),
        name="mlp_down",
    )(a, w, resid, w_gate, g_gate)


def _row_scale_kernel(ssq_ref, r_ref):
    s = ssq_ref[:, 0:LANES]
    for j in range(1, ssq_ref.shape[1] // LANES):
        s = s + ssq_ref[:, j * LANES:(j + 1) * LANES]
    r_ref[...] = lax.rsqrt(s * (1.0 / D_MODEL) + EPS)


def _row_scale(ssq, name, tm=1024):
    m, w = ssq.shape
    return pl.pallas_call(
        _row_scale_kernel,
        grid=(m // tm,),
        in_specs=[pl.BlockSpec((tm, w), lambda i: (i, 0))],
        out_specs=pl.BlockSpec((tm, LANES), lambda i: (i, 0)),
        out_shape=jax.ShapeDtypeStruct((m, LANES), F32),
        compiler_params=_params(("parallel",)),
        name=name,
    )(ssq)


def _pool_kernel(u_ref, w_ref, scale_ref, o_ref, ext_ref, sum_a_ref, sum_b_ref):
    ts = u_ref.shape[0]
    rows = POOL_HALO + ts
    i = pl.program_id(0)

    @pl.when(i == 0)
    def _():
        ext_ref[0:POOL_HALO, :] = jnp.zeros((POOL_HALO, D_POOL), F32)

    @pl.when(i != 0)
    def _():
        ext_ref[0:POOL_HALO, :] = ext_ref[ts:rows, :]

    ext_ref[POOL_HALO:rows, :] = u_ref[...]
    t = i * ts + lax.broadcasted_iota(jnp.int32, (ts, 1), 0)
    for g, win in enumerate(POOL_WINDOWS):
        cols = slice(g * POOL_GROUP, (g + 1) * POOL_GROUP)
        u = ext_ref[POOL_HALO:rows, cols]
        src, src_cols, lo, shift, stage = ext_ref, cols, 0, 1, 0
        while True:
            lo += SUBLANES
            sums = src[lo:rows, src_cols] + src[lo - shift:rows - shift, src_cols]
            shift *= 2
            if shift == win:
                break
            dst = (sum_a_ref, sum_b_ref)[stage % 2]
            dst[lo:rows, :] = sums
            src, src_cols, stage = dst, slice(None), stage + 1
        win_sum = sums[POOL_HALO - lo:, :]
        cnt = jnp.minimum(t + 1, win).astype(F32)
        z = (win_sum / cnt - u).astype(BF16)
        y = jnp.dot(z, w_ref[g], preferred_element_type=F32)
        o_ref[:, cols] = (y * scale_ref[:, cols]).astype(o_ref.dtype)


def _pool(proj, w_pool, scale, ts=512):
    s = proj.shape[0]
    return pl.pallas_call(
        _pool_kernel,
        grid=(s // ts,),
        in_specs=[
            pl.BlockSpec((ts, D_POOL), lambda i: (i, 0)),
            pl.BlockSpec((len(POOL_WINDOWS), POOL_GROUP, POOL_GROUP), lambda i: (0, 0, 0)),
            pl.BlockSpec((1, D_POOL), lambda i: (0, 0)),
        ],
        out_specs=pl.BlockSpec((ts, D_POOL), lambda i: (i, 0)),
        out_shape=jax.ShapeDtypeStruct((s, D_POOL), BF16),
        scratch_shapes=[pltpu.VMEM((ts + POOL_HALO, D_POOL), F32),
                        pltpu.VMEM((ts + POOL_HALO, POOL_GROUP), F32), pltpu.VMEM((ts + POOL_HALO, POOL_GROUP), F32)],
        compiler_params=_params(("arbitrary",)),
        name="pool_mixer",
    )(proj, w_pool, scale)


def _mlstm_kernel(qk_ref, v_ref, o_ref, li_ref, b_ref, mx_ref, lirow_ref, brow_ref,
                  cw_ref, cb_ref, gh_ref, y_ref, ext_ref, c_ref, n_ref, m_ref):
    L = CHUNK
    i = pl.program_id(0)

    @pl.when(i == 0)
    def _():
        ext_ref[0:CONV_HALO, :] = jnp.zeros((CONV_HALO, D_QK), F32)
        c_ref[...] = jnp.zeros_like(c_ref)
        n_ref[...] = jnp.zeros_like(n_ref)
        m_ref[...] = jnp.zeros_like(m_ref)

    @pl.when(i != 0)
    def _():
        ext_ref[0:CONV_HALO, :] = ext_ref[L:L + CONV_HALO, :]

    ext_ref[CONV_HALO:CONV_HALO + L, :] = qk_ref[...]

    t_idx = lax.broadcasted_iota(jnp.int32, (L, L), 0)
    s_idx = lax.broadcasted_iota(jnp.int32, (L, L), 1)
    causal = s_idx <= t_idx

    def conv_silu(cols):
        y = cb_ref[:, cols]
        for j in range(CONV_W):
            off = CONV_HALO - (CONV_W - 1) + j
            y = y + cw_ref[j:j + 1, cols] * ext_ref[off:off + L, cols]
        return y * _sigmoid(y)

    for h in range(N_HEADS):
        q = conv_silu(slice(h * DQK, (h + 1) * DQK))
        k = conv_silu(slice(N_HEADS * DQK + h * DQK, N_HEADS * DQK + (h + 1) * DQK)) * (DQK ** -0.5)
        qb = q.astype(BF16)
        kb = k.astype(BF16)
        vcols = slice(h * DV, (h + 1) * DV)
        v = v_ref[:, vcols].astype(BF16)
        li_col = li_ref[:, h:h + 1]
        b_col = b_ref[:, h:h + 1]
        mx_col = mx_ref[:, h:h + 1]
        li_row = lirow_ref[0, h:h + 1, :]
        b_row = brow_ref[0, h:h + 1, :]
        m_prev = m_ref[h][:, 0:1]
        c_prev = c_ref[h]
        n_prev = n_ref[h]

        d_mat = jnp.where(causal, b_col - b_row + li_row, -jnp.inf)
        e_col = b_col + m_prev
        m_t = jnp.maximum(mx_col, e_col)
        s_qk = lax.dot_general(qb, kb, (((1,), (1,)), ((), ())), preferred_element_type=F32)
        w_mat = jnp.exp(d_mat - m_t) * s_qk
        w_inter = jnp.exp(e_col - m_t)
        num = jnp.dot(w_mat.astype(BF16), v, preferred_element_type=F32)
        num = num + w_inter * jnp.dot(qb, c_prev.astype(BF16), preferred_element_type=F32)
        den = jnp.sum(w_mat, axis=-1, keepdims=True)
        den = den + w_inter * jnp.sum(q * n_prev, axis=-1, keepdims=True)
        inv = 1.0 / jnp.maximum(jnp.abs(den), jnp.exp(-m_t))
        msq = jnp.mean(num * num, axis=-1, keepdims=True)
        scale = inv * lax.rsqrt(inv * inv * msq + EPS)
        hh = num * scale * (gh_ref[:, vcols] * _sigmoid(o_ref[:, vcols]))
        y_ref[:, vcols] = hh.astype(y_ref.dtype)

        g_tot = b_col[L - 1:L, :]
        a_col = g_tot + li_col - b_col
        m_loc = jnp.max(a_col, axis=0, keepdims=True)
        kw = k * jnp.exp(a_col - m_loc)
        c_loc = lax.dot_general(kw.astype(BF16), v, (((0,), (0,)), ((), ())), preferred_element_type=F32)
        n_loc = jnp.sum(kw, axis=0, keepdims=True)
        m_new = jnp.maximum(g_tot + m_prev, m_loc)
        s_old = jnp.exp(g_tot + m_prev - m_new)
        s_loc = jnp.exp(m_loc - m_new)
        c_ref[h] = s_old * c_prev + s_loc * c_loc
        n_ref[h] = s_old * n_prev + s_loc * n_loc
        m_ref[h] = jnp.broadcast_to(m_new, (1, LANES))


def _mlstm(proj, li, b, mx, lirow, brow, conv_w, conv_b, g_head):
    s = proj.shape[0]
    L = CHUNK
    col_spec = pl.BlockSpec((L, LANES), lambda i: (i, 0))
    row_spec = pl.BlockSpec((1, N_HEADS, L), lambda i: (i, 0, 0))
    return pl.pallas_call(
        _mlstm_kernel,
        grid=(s // L,),
        in_specs=[
            pl.BlockSpec((L, D_QK), lambda i: (i, D_POOL // D_QK)),
            pl.BlockSpec((L, D_MLSTM), lambda i: (i, (D_POOL + D_QK) // D_MLSTM)),
            pl.BlockSpec((L, D_MLSTM), lambda i: (i, (D_POOL + D_QK + D_MLSTM) // D_MLSTM)),
            col_spec, col_spec, col_spec, row_spec, row_spec,
            pl.BlockSpec((CONV_W, D_QK), lambda i: (0, 0)),
            pl.BlockSpec((1, D_QK), lambda i: (0, 0)),
            pl.BlockSpec((1, D_MLSTM), lambda i: (0, 0)),
        ],
        out_specs=pl.BlockSpec((L, D_MLSTM), lambda i: (i, 0)),
        out_shape=jax.ShapeDtypeStruct((s, D_MLSTM), BF16),
        scratch_shapes=[
            pltpu.VMEM((L + CONV_HALO, D_QK), F32),
            pltpu.VMEM((N_HEADS, DQK, DV), F32),
            pltpu.VMEM((N_HEADS, 1, DQK), F32),
            pltpu.VMEM((N_HEADS, 1, LANES), F32),
        ],
        compiler_params=_params(("arbitrary",)),
        name="mlstm_mixer",
    )(proj, proj, proj, li, b, mx, lirow, brow, conv_w, conv_b, g_head)


def kernel(x, p, g_mix, w_in, conv_w, conv_b, b_igate, b_fgate, g_head, w_pool, pool_scale,
           w_out, g_mlp, w_up, w_down, g_ple, w_ple_gate, b_ple_gate, w_ple, g_final):
    depth = w_in.shape[0]
    h = x.reshape(SEQ, D_MODEL)
    for l in range(depth):
        w_g = jnp.pad(w_in[l, :, D_MAIN:], ((0, 0), (0, LANES - 2 * N_HEADS))).astype(BF16)
        gate_bias = jnp.pad(jnp.concatenate([b_igate[l], b_fgate[l]]), (0, LANES - 2 * N_HEADS)).reshape(1, LANES)

        hn, li, b, mx, lirow, brow = _norm_gates(h, g_mix[l].reshape(1, D_MODEL), w_g, gate_bias.astype(F32))
        proj = _proj_in(l, hn, jnp.swapaxes(w_in, 1, 2))
        y_pool = _pool(proj, w_pool[l].astype(BF16), pool_scale[l].reshape(1, D_POOL))
        y_mlstm = _mlstm(proj, li, b, mx, lirow, brow, conv_w[l], conv_b[l].reshape(1, D_QK),
                         g_head[l].reshape(1, D_MLSTM))
        h, hb, ssq = _mix_out(l, y_pool, y_mlstm, w_out, h)

        r = _row_scale(ssq, "scale_mlp")
        act, w_down_b = _mlp_up(l, hb, w_up, g_mlp[l].reshape(D_MODEL, 1), r, w_down)
        h, hb, ssq, w_gate_b = _down(l, act, w_down_b, h, w_ple_gate, g_ple[l].reshape(D_MODEL, 1))

        assert depth == 1
        h = _ple_final(hb, w_gate_b, ssq, b_ple_gate[l].reshape(1, D_MODEL), p[l].reshape(SEQ, D_PLE).astype(BF16),
                       w_ple[l].astype(BF16), h, g_final.reshape(1, D_MODEL))
    return h.reshape(x.shape)
```

```python
import jax
import jax.numpy as jnp
from jax import lax
from jax.experimental import pallas as pl
from jax.experimental.pallas import tpu as pltpu

F32 = jnp.float32
BF16 = jnp.bfloat16

D_MODEL = 4096
SEQ = 8192
D_POOL = 2048
POOL_WINDOWS = (2, 4, 8, 16)
POOL_GROUP = 512
SUBLANES = 8
POOL_HALO = 32
D_MLSTM = 2048
N_HEADS = 8
DV = 256
DQK = 128
D_QK = 2 * N_HEADS * DQK
CONV_W = 4
CONV_HALO = 8
GATE_CAP = 15.0
D_FF = 4 * D_MODEL
D_PLE = 256
EPS = 1e-6
D_MAIN = D_POOL + D_QK + 2 * D_MLSTM
LANES = 128
CHUNK = 256
VMEM_LIMIT = 58 * 1024 * 1024


def _params(sem, vmem=VMEM_LIMIT):
    return pltpu.CompilerParams(dimension_semantics=sem, vmem_limit_bytes=vmem)


def _chunk_scan(x, pos, combine, fill):
    sh = 1
    while sh < CHUNK:
        prev = jnp.where(pos >= sh, pltpu.roll(x, sh, 0), fill)
        x = combine(x, prev)
        sh *= 2
    return x


def _norm_gates_kernel(x_ref, g_ref, w_ref, bias_ref, hn_ref, li_ref, b_ref, mx_ref, lirow_ref, brow_ref):
    tm = x_ref.shape[0]
    x = x_ref[...]
    ms = jnp.mean(x * x, axis=-1, keepdims=True)
    hn = (x * lax.rsqrt(ms + EPS) * g_ref[...]).astype(BF16)
    hn_ref[...] = hn
    pre = jnp.dot(hn, w_ref[...], preferred_element_type=F32) + bias_ref[...]
    pre = GATE_CAP * jnp.tanh(pre / GATE_CAP)
    logf = jnp.minimum(pre, 0.0) - jnp.log1p(jnp.exp(-jnp.abs(pre)))
    pos = lax.broadcasted_iota(jnp.int32, (tm, LANES), 0) % CHUNK
    b = _chunk_scan(logf, pos, jnp.add, 0.0)
    b = pltpu.roll(b, LANES - N_HEADS, 1)
    li = pre
    mx = b + _chunk_scan(li - b, pos, jnp.maximum, -jnp.inf)
    li_ref[...] = li
    b_ref[...] = b
    mx_ref[...] = mx
    for c in range(tm // CHUNK):
        rows = slice(c * CHUNK, (c + 1) * CHUNK)
        lirow_ref[c] = li[rows].T[:N_HEADS]
        brow_ref[c] = b[rows].T[:N_HEADS]


def _norm_gates(x, g, w_g, bias, tm=512):
    m, d = x.shape
    col = jax.ShapeDtypeStruct((m, LANES), F32)
    row = jax.ShapeDtypeStruct((m // CHUNK, N_HEADS, CHUNK), F32)
    col_spec = pl.BlockSpec((tm, LANES), lambda i: (i, 0))
    row_spec = pl.BlockSpec((tm // CHUNK, N_HEADS, CHUNK), lambda i: (i, 0, 0))
    return pl.pallas_call(
        _norm_gates_kernel,
        grid=(m // tm,),
        in_specs=[
            pl.BlockSpec((tm, d), lambda i: (i, 0)),
            pl.BlockSpec((1, d), lambda i: (0, 0)),
            pl.BlockSpec((d, LANES), lambda i: (0, 0)),
            pl.BlockSpec((1, LANES), lambda i: (0, 0)),
        ],
        out_specs=[pl.BlockSpec((tm, d), lambda i: (i, 0)), col_spec, col_spec, col_spec, row_spec, row_spec],
        out_shape=[jax.ShapeDtypeStruct((m, d), BF16), col, col, col, row, row],
        compiler_params=_params(("parallel",)),
        name="norm_gates",
    )(x, g, w_g, bias)


def _ws_maps(nb, n_i):
    row = lambda b, i: jnp.where(b > 0, i, 0)
    blk = lambda b: jnp.maximum(b - 1, 0)
    wrow = lambda b, i: jnp.where(b < nb, i, n_i - 1)
    wblk = lambda b: jnp.minimum(b, nb - 1)
    return row, blk, wrow, wblk


def _ws_step(wchunk_ref, wbufs, compute):
    b = pl.program_id(0)
    i = pl.program_id(1)

    def stage(buf):
        ck = wchunk_ref.shape[0]
        buf[pl.ds(pl.multiple_of(i * ck, ck), ck), :] = wchunk_ref[...].astype(BF16)

    @pl.when(b == 0)
    def _():
        stage(wbufs[0])

    for parity in (0, 1):
        @pl.when(jnp.logical_and(b > 0, b % 2 == parity))
        def _():
            stage(wbufs[parity])
            compute(wbufs[1 - parity])


def _row_sumsq(h):
    return jnp.broadcast_to(jnp.sum(h * h, axis=1, keepdims=True), (h.shape[0], LANES))


def _lane_tile(r, n):
    return jnp.concatenate([r] * (n // LANES), axis=1)


def _sigmoid(x):
    return 0.5 * jnp.tanh(0.5 * x) + 0.5


def _proj_kernel(a_ref, wchunk_ref, o_ref, wbuf0, wbuf1):
    b = pl.program_id(0)
    i = pl.program_id(1)
    wbufs = (wbuf0, wbuf1)

    def stage(buf):
        w = wchunk_ref[...].T
        ck = w.shape[0]
        buf[pl.ds(pl.multiple_of(i * ck, ck), ck), :] = w.astype(BF16)

    @pl.when(b == 0)
    def _():
        stage(wbuf0)

    for parity in (0, 1):
        @pl.when(jnp.logical_and(b > 0, b % 2 == parity))
        def _():
            stage(wbufs[parity])
            o_ref[...] = jnp.dot(a_ref[...], wbufs[1 - parity][...], preferred_element_type=F32)

def _mix_out_kernel(yp_ref, ym_ref, wchunk_ref, res_ref, g_ref, o_ref, ob_ref, ssq_ref, wbuf0, wbuf1):
    def compute(w_ref):
        acc = jnp.dot(yp_ref[...], w_ref[0:D_POOL, :], preferred_element_type=F32)
        acc += jnp.dot(ym_ref[...], w_ref[D_POOL:D_POOL + D_MLSTM, :], preferred_element_type=F32)
        h = res_ref[...] + acc
        o_ref[...] = h
        ob_ref[...] = (h * g_ref[...]).astype(BF16)
        ssq_ref[...] = _row_sumsq(h)
    _ws_step(wchunk_ref, (wbuf0, wbuf1), compute)


def _up_kernel(a_ref, wchunk_ref, r_ref, wd_ref, o_ref, wdb_ref, wbuf0, wbuf1):
    def compute(w_ref):
        wdb_ref[...] = wd_ref[...].astype(BF16)
        y = jnp.dot(a_ref[...], w_ref[...], preferred_element_type=F32)
        y = jnp.maximum(y * _lane_tile(r_ref[...], y.shape[1]), 0.0)
        o_ref[...] = (y * y).astype(BF16)
    _ws_step(wchunk_ref, (wbuf0, wbuf1), compute)


def _ws_call(kern, name, l, a_list, w, k_dim, n_out, tm, tn, extra_in, extra_specs, out_shapes, out_specs):
    m = a_list[0].shape[0]
    nb, n_i = n_out // tn, m // tm
    ck = k_dim // n_i
    row, blk, wrow, wblk = _ws_maps(nb, n_i)
    a_specs = [pl.BlockSpec((tm, a.shape[1]), lambda b, i: (row(b, i), 0)) for a in a_list]
    w_spec = pl.BlockSpec((None, ck, tn), lambda b, i: (l, wrow(b, i), wblk(b)))
    return pl.pallas_call(
        kern,
        grid=(nb + 1, n_i),
        in_specs=a_specs + [w_spec] + extra_specs(row, blk, wrow, wblk),
        out_specs=out_specs(row, blk),
        out_shape=out_shapes,
        scratch_shapes=[pltpu.VMEM((k_dim, tn), BF16), pltpu.VMEM((k_dim, tn), BF16)],
        compiler_params=_params(("arbitrary", "arbitrary")),
        name=name,
    )(*a_list, w, *extra_in)


def _proj_in(l, hn, w_in_t, tm=1024, tn=1024):
    m = hn.shape[0]
    nb, n_i = D_MAIN // tn, m // tm
    ck = D_MODEL // n_i
    row, blk, wrow, wblk = _ws_maps(nb, n_i)
    return pl.pallas_call(
        _proj_kernel,
        grid=(nb + 1, n_i),
        in_specs=[
            pl.BlockSpec((tm, D_MODEL), lambda b, i: (row(b, i), 0)),
            pl.BlockSpec((None, tn, ck), lambda b, i: (l, wblk(b), wrow(b, i))),
        ],
        out_specs=pl.BlockSpec((tm, tn), lambda b, i: (row(b, i), blk(b))),
        out_shape=jax.ShapeDtypeStruct((m, D_MAIN), F32),
        scratch_shapes=[pltpu.VMEM((D_MODEL, tn), BF16), pltpu.VMEM((D_MODEL, tn), BF16)],
        compiler_params=_params(("arbitrary", "arbitrary")),
        name="proj_in",
    )(hn, w_in_t)


def _mix_out(l, y_pool, y_mlstm, w_out, resid, g_next, tm=512, tn=1024):
    m = resid.shape[0]
    nb = D_MODEL // tn
    tile = lambda row, blk: pl.BlockSpec((tm, tn), lambda b, i: (row(b, i), blk(b)))
    return _ws_call(
        _mix_out_kernel, "mix_out", l, [y_pool, y_mlstm], w_out, D_MODEL, D_MODEL, tm, tn, [resid, g_next],
        lambda row, blk, wrow, wblk: [tile(row, blk), pl.BlockSpec((1, tn), lambda b, i: (0, blk(b)))],
        [jax.ShapeDtypeStruct((m, D_MODEL), F32), jax.ShapeDtypeStruct((m, D_MODEL), BF16),
         jax.ShapeDtypeStruct((m, nb * LANES), F32)],
        lambda row, blk: [tile(row, blk), tile(row, blk),
                          pl.BlockSpec((tm, LANES), lambda b, i: (row(b, i), blk(b)))])


def _mlp_up(l, hb, w_up, r, w_down, tm=1024, tn=1024):
    m = hb.shape[0]
    nb, n_i = D_FF // tn, m // tm
    wd_rows = D_FF // (nb * n_i)
    side = lambda row, blk: (lambda b, i: (blk(b) * n_i + row(b, i), 0))
    return _ws_call(
        _up_kernel, "mlp_up", l, [hb], w_up, D_MODEL, D_FF, tm, tn, [r, w_down],
        lambda row, blk, wrow, wblk: [
            pl.BlockSpec((tm, LANES), lambda b, i: (row(b, i), 0)),
            pl.BlockSpec((None, wd_rows, D_MODEL), lambda b, i: (l,) + side(row, blk)(b, i)),
        ],
        [jax.ShapeDtypeStruct((m, D_FF), BF16), jax.ShapeDtypeStruct((D_FF, D_MODEL), BF16)],
        lambda row, blk: [pl.BlockSpec((tm, tn), lambda b, i: (row(b, i), blk(b))),
                          pl.BlockSpec((wd_rows, D_MODEL), side(row, blk))])


def _ple_kernel(a_ref, w_ref, ssq_ref, bias_ref, p_ref, wp_ref, res_hbm, gf_ref, o_ref, res_sem):
    tm, d = o_ref.shape
    rows = pl.ds(pl.multiple_of(pl.program_id(0) * tm, tm), tm)
    res_copy = pltpu.make_async_copy(res_hbm.at[rows, :], o_ref, res_sem)
    res_copy.start()
    s = ssq_ref[:, 0:LANES]
    for c in range(1, ssq_ref.shape[1] // LANES):
        s = s + ssq_ref[:, c * LANES:(c + 1) * LANES]
    r = lax.rsqrt(s * (1.0 / D_MODEL) + EPS)
    z = jnp.dot(a_ref[...], w_ref[...], preferred_element_type=F32)
    z = z * _lane_tile(r, d) + bias_ref[...]
    update = jax.nn.sigmoid(z) * jnp.dot(p_ref[...].astype(BF16), wp_ref[...], preferred_element_type=F32)
    res_copy.wait()
    x = o_ref[...] + update
    ms = jnp.mean(x * x, axis=-1, keepdims=True)
    o_ref[...] = x * lax.rsqrt(ms + EPS) * gf_ref[...]


def _ple_final(hb, w_gate_b, ssq, b_gate, p, w_ple, resid, g_final, tm=256):
    m = hb.shape[0]
    resident = pl.Buffered(1)
    return pl.pallas_call(
        _ple_kernel,
        grid=(m // tm,),
        in_specs=[
            pl.BlockSpec((tm, D_MODEL), lambda i: (i, 0)),
            pl.BlockSpec((D_MODEL, D_MODEL), lambda i: (0, 0), pipeline_mode=resident),
            pl.BlockSpec((tm, ssq.shape[1]), lambda i: (i, 0)),
            pl.BlockSpec((1, D_MODEL), lambda i: (0, 0)),
            pl.BlockSpec((tm, D_PLE), lambda i: (i, 0)),
            pl.BlockSpec((D_PLE, D_MODEL), lambda i: (0, 0), pipeline_mode=resident),
            pl.BlockSpec(memory_space=pl.ANY),
            pl.BlockSpec((1, D_MODEL), lambda i: (0, 0)),
        ],
        out_specs=pl.BlockSpec((tm, D_MODEL), lambda i: (i, 0)),
        out_shape=jax.ShapeDtypeStruct((m, D_MODEL), F32),
        scratch_shapes=[pltpu.SemaphoreType.DMA(())],
        compiler_params=_params(("arbitrary",)),
        name="ple_final",
    )(hb, w_gate_b, ssq, b_gate, p, w_ple, resid, g_final)


def _down_kernel(a_ref, w_ref, res_hbm, wg_ref, g_ref, o_ref, ob_ref, ssq_ref, wgb_ref, res_sem):
    i, j, k = pl.program_id(0), pl.program_id(1), pl.program_id(2)
    last = pl.num_programs(2) - 1
    tm, tn = o_ref.shape
    wgb_ref[...] = wg_ref[...].astype(BF16)

    def part():
        return jnp.dot(a_ref[...], w_ref[...], preferred_element_type=F32)

    @pl.when(k == 0)
    def _():
        rows = pl.ds(pl.multiple_of(i * tm, tm), tm)
        cols = pl.ds(pl.multiple_of(j * tn, tn), tn)
        res_copy = pltpu.make_async_copy(res_hbm.at[rows, cols], o_ref, res_sem)
        res_copy.start()
        p = part()
        res_copy.wait()
        o_ref[...] += p

    @pl.when(jnp.logical_and(k != 0, k != last))
    def _():
        o_ref[...] += part()

    @pl.when(k == last)
    def _():
        h = o_ref[...] + part()
        o_ref[...] = h
        ob_ref[...] = (h * g_ref[...]).astype(BF16)
        ssq_ref[...] = _row_sumsq(h)


def _down(l, a, w, resid, w_gate, g_next, tm=1024, tn=1024, tk=4096):
    m, k = a.shape
    n = w.shape[1]
    n_j, n_k = n // tn, k // tk
    side_rows = w_gate.shape[1] // ((m // tm) * n_j * n_k)
    tile = pl.BlockSpec((tm, tn), lambda i, j, kk: (i, j))
    step = lambda i, j, kk: (i * n_j + j) * n_k + kk
    return pl.pallas_call(
        _down_kernel,
        grid=(m // tm, n_j, n_k),
        in_specs=[
            pl.BlockSpec((tm, tk), lambda i, j, kk: (i, kk)),
            pl.BlockSpec((tk, tn), lambda i, j, kk: (kk, j)),
            pl.BlockSpec(memory_space=pl.ANY),
            pl.BlockSpec((None, side_rows, w_gate.shape[2]), lambda i, j, kk: (l, step(i, j, kk), 0)),
            pl.BlockSpec((1, tn), lambda i, j, kk: (0, j)),
        ],
        out_specs=[tile, tile, pl.BlockSpec((tm, LANES), lambda i, j, kk: (i, j)),
                   pl.BlockSpec((side_rows, w_gate.shape[2]), lambda i, j, kk: (step(i, j, kk), 0))],
        out_shape=[jax.ShapeDtypeStruct((m, n), F32), jax.ShapeDtypeStruct((m, n), BF16),
                   jax.ShapeDtypeStruct((m, n_j * LANES), F32),
                   jax.ShapeDtypeStruct(w_gate.shape[1:], BF16)],
        scratch_shapes=[pltpu.SemaphoreType.DMA(())],
        compiler_params=_params(("arbitrary", "arbitrary", "arbitrary")),
        name="mlp_down",
    )(a, w, resid, w_gate, g_next)


def _row_scale_kernel(ssq_ref, r_ref):
    s = ssq_ref[:, 0:LANES]
    for j in range(1, ssq_ref.shape[1] // LANES):
        s = s + ssq_ref[:, j * LANES:(j + 1) * LANES]
    r_ref[...] = lax.rsqrt(s * (1.0 / D_MODEL) + EPS)


def _row_scale(ssq, name, tm=1024):
    m, w = ssq.shape
    return pl.pallas_call(
        _row_scale_kernel,
        grid=(m // tm,),
        in_specs=[pl.BlockSpec((tm, w), lambda i: (i, 0))],
        out_specs=pl.BlockSpec((tm, LANES), lambda i: (i, 0)),
        out_shape=jax.ShapeDtypeStruct((m, LANES), F32),
        compiler_params=_params(("parallel",)),
        name=name,
    )(ssq)


def _pool_kernel(u_ref, w_ref, scale_ref, o_ref, ext_ref, sum_a_ref, sum_b_ref):
    ts = u_ref.shape[0]
    rows = POOL_HALO + ts
    i = pl.program_id(0)

    @pl.when(i == 0)
    def _():
        ext_ref[0:POOL_HALO, :] = jnp.zeros((POOL_HALO, D_POOL), F32)

    @pl.when(i != 0)
    def _():
        ext_ref[0:POOL_HALO, :] = ext_ref[ts:rows, :]

    ext_ref[POOL_HALO:rows, :] = u_ref[...]
    t = i * ts + lax.broadcasted_iota(jnp.int32, (ts, 1), 0)
    for g, win in enumerate(POOL_WINDOWS):
        cols = slice(g * POOL_GROUP, (g + 1) * POOL_GROUP)
        u = ext_ref[POOL_HALO:rows, cols]
        src, src_cols, lo, shift, stage = ext_ref, cols, 0, 1, 0
        while True:
            lo += SUBLANES
            sums = src[lo:rows, src_cols] + src[lo - shift:rows - shift, src_cols]
            shift *= 2
            if shift == win:
                break
            dst = (sum_a_ref, sum_b_ref)[stage % 2]
            dst[lo:rows, :] = sums
            src, src_cols, stage = dst, slice(None), stage + 1
        win_sum = sums[POOL_HALO - lo:, :]
        cnt = jnp.minimum(t + 1, win).astype(F32)
        z = (win_sum / cnt - u).astype(BF16)
        y = jnp.dot(z, w_ref[g], preferred_element_type=F32)
        o_ref[:, cols] = (y * scale_ref[:, cols]).astype(o_ref.dtype)


def _pool(proj, w_pool, scale, ts=512):
    s = proj.shape[0]
    return pl.pallas_call(
        _pool_kernel,
        grid=(s // ts,),
        in_specs=[
            pl.BlockSpec((ts, D_POOL), lambda i: (i, 0)),
            pl.BlockSpec((len(POOL_WINDOWS), POOL_GROUP, POOL_GROUP), lambda i: (0, 0, 0)),
            pl.BlockSpec((1, D_POOL), lambda i: (0, 0)),
        ],
        out_specs=pl.BlockSpec((ts, D_POOL), lambda i: (i, 0)),
        out_shape=jax.ShapeDtypeStruct((s, D_POOL), BF16),
        scratch_shapes=[pltpu.VMEM((ts + POOL_HALO, D_POOL), F32),
                        pltpu.VMEM((ts + POOL_HALO, POOL_GROUP), F32), pltpu.VMEM((ts + POOL_HALO, POOL_GROUP), F32)],
        compiler_params=_params(("arbitrary",)),
        name="pool_mixer",
    )(proj, w_pool, scale)


def _mlstm_kernel(qk_ref, v_ref, o_ref, li_ref, b_ref, mx_ref, lirow_ref, brow_ref,
                  cw_ref, cb_ref, gh_ref, y_ref, ext_ref, c_ref, n_ref, m_ref):
    L = CHUNK
    i = pl.program_id(0)

    @pl.when(i == 0)
    def _():
        ext_ref[0:CONV_HALO, :] = jnp.zeros((CONV_HALO, D_QK), F32)
        c_ref[...] = jnp.zeros_like(c_ref)
        n_ref[...] = jnp.zeros_like(n_ref)
        m_ref[...] = jnp.zeros_like(m_ref)

    @pl.when(i != 0)
    def _():
        ext_ref[0:CONV_HALO, :] = ext_ref[L:L + CONV_HALO, :]

    ext_ref[CONV_HALO:CONV_HALO + L, :] = qk_ref[...]

    t_idx = lax.broadcasted_iota(jnp.int32, (L, L), 0)
    s_idx = lax.broadcasted_iota(jnp.int32, (L, L), 1)
    causal = s_idx <= t_idx

    def conv_silu(cols):
        y = cb_ref[:, cols]
        for j in range(CONV_W):
            off = CONV_HALO - (CONV_W - 1) + j
            y = y + cw_ref[j:j + 1, cols] * ext_ref[off:off + L, cols]
        return y * _sigmoid(y)

    for h in range(N_HEADS):
        q = conv_silu(slice(h * DQK, (h + 1) * DQK))
        k = conv_silu(slice(N_HEADS * DQK + h * DQK, N_HEADS * DQK + (h + 1) * DQK)) * (DQK ** -0.5)
        qb = q.astype(BF16)
        kb = k.astype(BF16)
        vcols = slice(h * DV, (h + 1) * DV)
        v = v_ref[:, vcols].astype(BF16)
        li_col = li_ref[:, h:h + 1]
        b_col = b_ref[:, h:h + 1]
        mx_col = mx_ref[:, h:h + 1]
        li_row = lirow_ref[0, h:h + 1, :]
        b_row = brow_ref[0, h:h + 1, :]
        m_prev = m_ref[h][:, 0:1]
        c_prev = c_ref[h]
        n_prev = n_ref[h]

        d_mat = jnp.where(causal, b_col - b_row + li_row, -jnp.inf)
        e_col = b_col + m_prev
        m_t = jnp.maximum(mx_col, e_col)
        s_qk = lax.dot_general(qb, kb, (((1,), (1,)), ((), ())), preferred_element_type=F32)
        w_mat = jnp.exp(d_mat - m_t) * s_qk
        w_inter = jnp.exp(e_col - m_t)
        num = jnp.dot(w_mat.astype(BF16), v, preferred_element_type=F32)
        num = num + w_inter * jnp.dot(qb, c_prev.astype(BF16), preferred_element_type=F32)
        den = jnp.sum(w_mat, axis=-1, keepdims=True)
        den = den + w_inter * jnp.sum(q * n_prev, axis=-1, keepdims=True)
        inv = 1.0 / jnp.maximum(jnp.abs(den), jnp.exp(-m_t))
        msq = jnp.mean(num * num, axis=-1, keepdims=True)
        scale = inv * lax.rsqrt(inv * inv * msq + EPS)
        hh = num * scale * (gh_ref[:, vcols] * _sigmoid(o_ref[:, vcols]))
        y_ref[:, vcols] = hh.astype(y_ref.dtype)

        g_tot = b_col[L - 1:L, :]
        a_col = g_tot + li_col - b_col
        m_loc = jnp.max(a_col, axis=0, keepdims=True)
        kw = k * jnp.exp(a_col - m_loc)
        c_loc = lax.dot_general(kw.astype(BF16), v, (((0,), (0,)), ((), ())), preferred_element_type=F32)
        n_loc = jnp.sum(kw, axis=0, keepdims=True)
        m_new = jnp.maximum(g_tot + m_prev, m_loc)
        s_old = jnp.exp(g_tot + m_prev - m_new)
        s_loc = jnp.exp(m_loc - m_new)
        c_ref[h] = s_old * c_prev + s_loc * c_loc
        n_ref[h] = s_old * n_prev + s_loc * n_loc
        m_ref[h] = jnp.broadcast_to(m_new, (1, LANES))


def _mlstm(proj, li, b, mx, lirow, brow, conv_w, conv_b, g_head):
    s = proj.shape[0]
    L = CHUNK
    col_spec = pl.BlockSpec((L, LANES), lambda i: (i, 0))
    row_spec = pl.BlockSpec((1, N_HEADS, L), lambda i: (i, 0, 0))
    return pl.pallas_call(
        _mlstm_kernel,
        grid=(s // L,),
        in_specs=[
            pl.BlockSpec((L, D_QK), lambda i: (i, D_POOL // D_QK)),
            pl.BlockSpec((L, D_MLSTM), lambda i: (i, (D_POOL + D_QK) // D_MLSTM)),
            pl.BlockSpec((L, D_MLSTM), lambda i: (i, (D_POOL + D_QK + D_MLSTM) // D_MLSTM)),
            col_spec, col_spec, col_spec, row_spec, row_spec,
            pl.BlockSpec((CONV_W, D_QK), lambda i: (0, 0)),
            pl.BlockSpec((1, D_QK), lambda i: (0, 0)),
            pl.BlockSpec((1, D_MLSTM), lambda i: (0, 0)),
        ],
        out_specs=pl.BlockSpec((L, D_MLSTM), lambda i: (i, 0)),
        out_shape=jax.ShapeDtypeStruct((s, D_MLSTM), BF16),
        scratch_shapes=[
            pltpu.VMEM((L + CONV_HALO, D_QK), F32),
            pltpu.VMEM((N_HEADS, DQK, DV), F32),
            pltpu.VMEM((N_HEADS, 1, DQK), F32),
            pltpu.VMEM((N_HEADS, 1, LANES), F32),
        ],
        compiler_params=_params(("arbitrary",)),
        name="mlstm_mixer",
    )(proj, proj, proj, li, b, mx, lirow, brow, conv_w, conv_b, g_head)


def kernel(x, p, g_mix, w_in, conv_w, conv_b, b_igate, b_fgate, g_head, w_pool, pool_scale,
           w_out, g_mlp, w_up, w_down, g_ple, w_ple_gate, b_ple_gate, w_ple, g_final):
    depth = w_in.shape[0]
    h = x.reshape(SEQ, D_MODEL)
    for l in range(depth):
        w_g = jnp.pad(w_in[l, :, D_MAIN:], ((0, 0), (0, LANES - 2 * N_HEADS))).astype(BF16)
        gate_bias = jnp.pad(jnp.concatenate([b_igate[l], b_fgate[l]]), (0, LANES - 2 * N_HEADS)).reshape(1, LANES)

        hn, li, b, mx, lirow, brow = _norm_gates(h, g_mix[l].reshape(1, D_MODEL), w_g, gate_bias.astype(F32))
        proj = _proj_in(l, hn, jnp.swapaxes(w_in, 1, 2))
        y_pool = _pool(proj, w_pool[l].astype(BF16), pool_scale[l].reshape(1, D_POOL))
        y_mlstm = _mlstm(proj, li, b, mx, lirow, brow, conv_w[l], conv_b[l].reshape(1, D_QK),
                         g_head[l].reshape(1, D_MLSTM))
        h, hb, ssq = _mix_out(l, y_pool, y_mlstm, w_out, h, g_mlp[l].reshape(1, D_MODEL))
        r = _row_scale(ssq, "scale_mlp")
        act, w_down_b = _mlp_up(l, hb, w_up, r, w_down)
        h, hb, ssq, w_gate_b = _down(l, act, w_down_b, h, w_ple_gate, g_ple[l].reshape(1, D_MODEL))

        assert depth == 1
        h = _ple_final(hb, w_gate_b, ssq, b_ple_gate[l].reshape(1, D_MODEL), p[l].reshape(SEQ, D_PLE),
                       w_ple[l].astype(BF16), h, g_final.reshape(1, D_MODEL))
    return h.reshape(x.shape)
```

```python
import jax
import jax.numpy as jnp
from jax import lax
from jax.experimental import pallas as pl
from jax.experimental.pallas import tpu as pltpu

F32 = jnp.float32
BF16 = jnp.bfloat16

D_MODEL = 4096
SEQ = 8192
D_POOL = 2048
POOL_WINDOWS = (2, 4, 8, 16)
POOL_GROUP = 512
SUBLANES = 8
POOL_HALO = 32
D_MLSTM = 2048
N_HEADS = 8
DV = 256
DQK = 128
D_QK = 2 * N_HEADS * DQK
CONV_W = 4
CONV_HALO = 8
GATE_CAP = 15.0
D_FF = 4 * D_MODEL
D_PLE = 256
EPS = 1e-6
D_MAIN = D_POOL + D_QK + 2 * D_MLSTM
LANES = 128
CHUNK = 256
VMEM_LIMIT = 58 * 1024 * 1024


def _params(sem, vmem=VMEM_LIMIT):
    return pltpu.CompilerParams(dimension_semantics=sem, vmem_limit_bytes=vmem)


def _chunk_scan(x, pos, combine, fill):
    sh = 1
    while sh < CHUNK:
        prev = jnp.where(pos >= sh, pltpu.roll(x, sh, 0), fill)
        x = combine(x, prev)
        sh *= 2
    return x


def _norm_gates_kernel(x_ref, g_ref, w_ref, bias_ref, hn_ref, li_ref, b_ref, mx_ref, lirow_ref, brow_ref):
    tm = x_ref.shape[0]
    x = x_ref[...]
    ms = jnp.mean(x * x, axis=-1, keepdims=True)
    hn = (x * lax.rsqrt(ms + EPS) * g_ref[...]).astype(BF16)
    hn_ref[...] = hn
    pre = jnp.dot(hn, w_ref[...], preferred_element_type=F32) + bias_ref[...]
    pre = GATE_CAP * jnp.tanh(pre / GATE_CAP)
    logf = jnp.minimum(pre, 0.0) - jnp.log1p(jnp.exp(-jnp.abs(pre)))
    pos = lax.broadcasted_iota(jnp.int32, (tm, LANES), 0) % CHUNK
    b = _chunk_scan(logf, pos, jnp.add, 0.0)
    b = pltpu.roll(b, LANES - N_HEADS, 1)
    li = pre
    mx = b + _chunk_scan(li - b, pos, jnp.maximum, -jnp.inf)
    li_ref[...] = li
    b_ref[...] = b
    mx_ref[...] = mx
    for c in range(tm // CHUNK):
        rows = slice(c * CHUNK, (c + 1) * CHUNK)
        lirow_ref[c] = li[rows].T[:N_HEADS]
        brow_ref[c] = b[rows].T[:N_HEADS]


def _norm_gates(x, g, w_g, bias, tm=512):
    m, d = x.shape
    col = jax.ShapeDtypeStruct((m, LANES), F32)
    row = jax.ShapeDtypeStruct((m // CHUNK, N_HEADS, CHUNK), F32)
    col_spec = pl.BlockSpec((tm, LANES), lambda i: (i, 0))
    row_spec = pl.BlockSpec((tm // CHUNK, N_HEADS, CHUNK), lambda i: (i, 0, 0))
    return pl.pallas_call(
        _norm_gates_kernel,
        grid=(m // tm,),
        in_specs=[
            pl.BlockSpec((tm, d), lambda i: (i, 0)),
            pl.BlockSpec((1, d), lambda i: (0, 0)),
            pl.BlockSpec((d, LANES), lambda i: (0, 0)),
            pl.BlockSpec((1, LANES), lambda i: (0, 0)),
        ],
        out_specs=[pl.BlockSpec((tm, d), lambda i: (i, 0)), col_spec, col_spec, col_spec, row_spec, row_spec],
        out_shape=[jax.ShapeDtypeStruct((m, d), BF16), col, col, col, row, row],
        compiler_params=_params(("parallel",)),
        name="norm_gates",
    )(x, g, w_g, bias)


def _ws_maps(nb, n_i):
    row = lambda b, i: jnp.where(b > 0, i, 0)
    blk = lambda b: jnp.maximum(b - 1, 0)
    wrow = lambda b, i: jnp.where(b < nb, i, n_i - 1)
    wblk = lambda b: jnp.minimum(b, nb - 1)
    return row, blk, wrow, wblk


def _ws_step(wchunk_ref, wbufs, compute):
    b = pl.program_id(0)
    i = pl.program_id(1)

    def stage(buf):
        ck = wchunk_ref.shape[0]
        buf[pl.ds(pl.multiple_of(i * ck, ck), ck), :] = wchunk_ref[...].astype(BF16)

    @pl.when(b == 0)
    def _():
        stage(wbufs[0])

    for parity in (0, 1):
        @pl.when(jnp.logical_and(b > 0, b % 2 == parity))
        def _():
            stage(wbufs[parity])
            compute(wbufs[1 - parity])


def _row_sumsq(h):
    return jnp.broadcast_to(jnp.sum(h * h, axis=1, keepdims=True), (h.shape[0], LANES))


def _lane_tile(r, n):
    return jnp.concatenate([r] * (n // LANES), axis=1)


def _sigmoid(x):
    return 0.5 * jnp.tanh(0.5 * x) + 0.5


def _proj_kernel(a_ref, wchunk_ref, o_ref, wbuf0, wbuf1):
    b = pl.program_id(0)
    i = pl.program_id(1)
    wbufs = (wbuf0, wbuf1)

    def stage(buf):
        w = wchunk_ref[...].T
        ck = w.shape[0]
        buf[pl.ds(pl.multiple_of(i * ck, ck), ck), :] = w.astype(BF16)

    @pl.when(b == 0)
    def _():
        stage(wbuf0)

    for parity in (0, 1):
        @pl.when(jnp.logical_and(b > 0, b % 2 == parity))
        def _():
            stage(wbufs[parity])
            o_ref[...] = jnp.dot(a_ref[...], wbufs[1 - parity][...], preferred_element_type=F32)

def _mix_out_kernel(yp_ref, ym_ref, wchunk_ref, res_ref, g_ref, o_ref, ob_ref, r_ref, wbuf0, wbuf1, ssq_ref):
    b = pl.program_id(0)
    tm = o_ref.shape[0]
    rows = pl.ds(pl.multiple_of(pl.program_id(1) * tm, tm), tm)

    @pl.when(b == 0)
    def _():
        ssq_ref[rows, :] = jnp.zeros((tm, LANES), F32)

    def compute(w_ref):
        acc = jnp.dot(yp_ref[...], w_ref[0:D_POOL, :], preferred_element_type=F32)
        acc += jnp.dot(ym_ref[...], w_ref[D_POOL:D_POOL + D_MLSTM, :], preferred_element_type=F32)
        h = res_ref[...] + acc
        o_ref[...] = h
        ob_ref[...] = (h * g_ref[...]).astype(BF16)
        total = ssq_ref[rows, :] + _row_sumsq(h)
        ssq_ref[rows, :] = total

        @pl.when(b == pl.num_programs(0) - 1)
        def _():
            r_ref[...] = lax.rsqrt(total * (1.0 / D_MODEL) + EPS)
    _ws_step(wchunk_ref, (wbuf0, wbuf1), compute)


def _up_kernel(a_ref, wchunk_ref, r_ref, wd_ref, o_ref, wdb_ref, wbuf0, wbuf1):
    def compute(w_ref):
        wdb_ref[...] = wd_ref[...].astype(BF16)
        y = jnp.dot(a_ref[...], w_ref[...], preferred_element_type=F32)
        y = jnp.maximum(y * _lane_tile(r_ref[...], y.shape[1]), 0.0)
        o_ref[...] = (y * y).astype(BF16)
    _ws_step(wchunk_ref, (wbuf0, wbuf1), compute)


def _ws_call(kern, name, l, a_list, w, k_dim, n_out, tm, tn, extra_in, extra_specs, out_shapes, out_specs,
             extra_scratch=()):
    m = a_list[0].shape[0]
    nb, n_i = n_out // tn, m // tm
    ck = k_dim // n_i
    row, blk, wrow, wblk = _ws_maps(nb, n_i)
    a_specs = [pl.BlockSpec((tm, a.shape[1]), lambda b, i: (row(b, i), 0)) for a in a_list]
    w_spec = pl.BlockSpec((None, ck, tn), lambda b, i: (l, wrow(b, i), wblk(b)))
    return pl.pallas_call(
        kern,
        grid=(nb + 1, n_i),
        in_specs=a_specs + [w_spec] + extra_specs(row, blk, wrow, wblk),
        out_specs=out_specs(row, blk),
        out_shape=out_shapes,
        scratch_shapes=[pltpu.VMEM((k_dim, tn), BF16), pltpu.VMEM((k_dim, tn), BF16), *extra_scratch],
        compiler_params=_params(("arbitrary", "arbitrary")),
        name=name,
    )(*a_list, w, *extra_in)


def _proj_in(l, hn, w_in_t, tm=1024, tn=1024):
    m = hn.shape[0]
    nb, n_i = D_MAIN // tn, m // tm
    ck = D_MODEL // n_i
    row, blk, wrow, wblk = _ws_maps(nb, n_i)
    return pl.pallas_call(
        _proj_kernel,
        grid=(nb + 1, n_i),
        in_specs=[
            pl.BlockSpec((tm, D_MODEL), lambda b, i: (row(b, i), 0)),
            pl.BlockSpec((None, tn, ck), lambda b, i: (l, wblk(b), wrow(b, i))),
        ],
        out_specs=pl.BlockSpec((tm, tn), lambda b, i: (row(b, i), blk(b))),
        out_shape=jax.ShapeDtypeStruct((m, D_MAIN), F32),
        scratch_shapes=[pltpu.VMEM((D_MODEL, tn), BF16), pltpu.VMEM((D_MODEL, tn), BF16)],
        compiler_params=_params(("arbitrary", "arbitrary")),
        name="proj_in",
    )(hn, w_in_t)


def _mix_out(l, y_pool, y_mlstm, w_out, resid, g_next, tm=512, tn=1024):
    m = resid.shape[0]
    nb = D_MODEL // tn
    tile = lambda row, blk: pl.BlockSpec((tm, tn), lambda b, i: (row(b, i), blk(b)))
    return _ws_call(
        _mix_out_kernel, "mix_out", l, [y_pool, y_mlstm], w_out, D_MODEL, D_MODEL, tm, tn, [resid, g_next],
        lambda row, blk, wrow, wblk: [tile(row, blk), pl.BlockSpec((1, tn), lambda b, i: (0, blk(b)))],
        [jax.ShapeDtypeStruct((m, D_MODEL), F32), jax.ShapeDtypeStruct((m, D_MODEL), BF16),
         jax.ShapeDtypeStruct((m, LANES), F32)],
        lambda row, blk: [tile(row, blk), tile(row, blk),
                          pl.BlockSpec((tm, LANES), lambda b, i: (jnp.where(b == nb, i, 0), 0))],
        extra_scratch=[pltpu.VMEM((m, LANES), F32)])


def _mlp_up(l, hb, w_up, r, w_down, tm=1024, tn=1024):
    m = hb.shape[0]
    nb, n_i = D_FF // tn, m // tm
    wd_rows = D_FF // (nb * n_i)
    side = lambda row, blk: (lambda b, i: (blk(b) * n_i + row(b, i), 0))
    return _ws_call(
        _up_kernel, "mlp_up", l, [hb], w_up, D_MODEL, D_FF, tm, tn, [r, w_down],
        lambda row, blk, wrow, wblk: [
            pl.BlockSpec((tm, LANES), lambda b, i: (row(b, i), 0)),
            pl.BlockSpec((None, wd_rows, D_MODEL), lambda b, i: (l,) + side(row, blk)(b, i)),
        ],
        [jax.ShapeDtypeStruct((m, D_FF), BF16), jax.ShapeDtypeStruct((D_FF, D_MODEL), BF16)],
        lambda row, blk: [pl.BlockSpec((tm, tn), lambda b, i: (row(b, i), blk(b))),
                          pl.BlockSpec((wd_rows, D_MODEL), side(row, blk))])


def _ple_kernel(a_ref, w_ref, ssq_ref, bias_ref, p_ref, wp_ref, res_hbm, gf_ref, o_ref, res_sem):
    tm, d = o_ref.shape
    rows = pl.ds(pl.multiple_of(pl.program_id(0) * tm, tm), tm)
    res_copy = pltpu.make_async_copy(res_hbm.at[rows, :], o_ref, res_sem)
    res_copy.start()
    s = ssq_ref[:, 0:LANES]
    for c in range(1, ssq_ref.shape[1] // LANES):
        s = s + ssq_ref[:, c * LANES:(c + 1) * LANES]
    r = lax.rsqrt(s * (1.0 / D_MODEL) + EPS)
    z = jnp.dot(a_ref[...], w_ref[...], preferred_element_type=F32)
    z = z * _lane_tile(r, d) + bias_ref[...]
    update = jax.nn.sigmoid(z) * jnp.dot(p_ref[...].astype(BF16), wp_ref[...], preferred_element_type=F32)
    res_copy.wait()
    x = o_ref[...] + update
    ms = jnp.mean(x * x, axis=-1, keepdims=True)
    o_ref[...] = x * lax.rsqrt(ms + EPS) * gf_ref[...]


def _ple_final(hb, w_gate_b, ssq, b_gate, p, w_ple, resid, g_final, tm=256):
    m = hb.shape[0]
    resident = pl.Buffered(1)
    return pl.pallas_call(
        _ple_kernel,
        grid=(m // tm,),
        in_specs=[
            pl.BlockSpec((tm, D_MODEL), lambda i: (i, 0)),
            pl.BlockSpec((D_MODEL, D_MODEL), lambda i: (0, 0), pipeline_mode=resident),
            pl.BlockSpec((tm, ssq.shape[1]), lambda i: (i, 0)),
            pl.BlockSpec((1, D_MODEL), lambda i: (0, 0)),
            pl.BlockSpec((tm, D_PLE), lambda i: (i, 0)),
            pl.BlockSpec((D_PLE, D_MODEL), lambda i: (0, 0), pipeline_mode=resident),
            pl.BlockSpec(memory_space=pl.ANY),
            pl.BlockSpec((1, D_MODEL), lambda i: (0, 0)),
        ],
        out_specs=pl.BlockSpec((tm, D_MODEL), lambda i: (i, 0)),
        out_shape=jax.ShapeDtypeStruct((m, D_MODEL), F32),
        scratch_shapes=[pltpu.SemaphoreType.DMA(())],
        compiler_params=_params(("arbitrary",)),
        name="ple_final",
    )(hb, w_gate_b, ssq, b_gate, p, w_ple, resid, g_final)


def _down_kernel(a_ref, w_ref, res_hbm, wg_ref, g_ref, o_ref, ob_ref, ssq_ref, wgb_ref, res_sem):
    i, j, k = pl.program_id(0), pl.program_id(1), pl.program_id(2)
    last = pl.num_programs(2) - 1
    tm, tn = o_ref.shape
    wgb_ref[...] = wg_ref[...].astype(BF16)

    def part():
        return jnp.dot(a_ref[...], w_ref[...], preferred_element_type=F32)

    @pl.when(k == 0)
    def _():
        rows = pl.ds(pl.multiple_of(i * tm, tm), tm)
        cols = pl.ds(pl.multiple_of(j * tn, tn), tn)
        res_copy = pltpu.make_async_copy(res_hbm.at[rows, cols], o_ref, res_sem)
        res_copy.start()
        p = part()
        res_copy.wait()
        o_ref[...] += p

    @pl.when(jnp.logical_and(k != 0, k != last))
    def _():
        o_ref[...] += part()

    @pl.when(k == last)
    def _():
        h = o_ref[...] + part()
        o_ref[...] = h
        ob_ref[...] = (h * g_ref[...]).astype(BF16)
        ssq_ref[...] = _row_sumsq(h)


def _down(l, a, w, resid, w_gate, g_next, tm=1024, tn=1024, tk=4096):
    m, k = a.shape
    n = w.shape[1]
    n_j, n_k = n // tn, k // tk
    side_rows = w_gate.shape[1] // ((m // tm) * n_j * n_k)
    tile = pl.BlockSpec((tm, tn), lambda i, j, kk: (i, j))
    step = lambda i, j, kk: (i * n_j + j) * n_k + kk
    return pl.pallas_call(
        _down_kernel,
        grid=(m // tm, n_j, n_k),
        in_specs=[
            pl.BlockSpec((tm, tk), lambda i, j, kk: (i, kk)),
            pl.BlockSpec((tk, tn), lambda i, j, kk: (kk, j)),
            pl.BlockSpec(memory_space=pl.ANY),
            pl.BlockSpec((None, side_rows, w_gate.shape[2]), lambda i, j, kk: (l, step(i, j, kk), 0)),
            pl.BlockSpec((1, tn), lambda i, j, kk: (0, j)),
        ],
        out_specs=[tile, tile, pl.BlockSpec((tm, LANES), lambda i, j, kk: (i, j)),
                   pl.BlockSpec((side_rows, w_gate.shape[2]), lambda i, j, kk: (step(i, j, kk), 0))],
        out_shape=[jax.ShapeDtypeStruct((m, n), F32), jax.ShapeDtypeStruct((m, n), BF16),
                   jax.ShapeDtypeStruct((m, n_j * LANES), F32),
                   jax.ShapeDtypeStruct(w_gate.shape[1:], BF16)],
        scratch_shapes=[pltpu.SemaphoreType.DMA(())],
        compiler_params=_params(("arbitrary", "arbitrary", "arbitrary")),
        name="mlp_down",
    )(a, w, resid, w_gate, g_next)


def _pool_kernel(u_ref, w_ref, scale_ref, o_ref, ext_ref, sum_a_ref, sum_b_ref):
    ts = u_ref.shape[0]
    rows = POOL_HALO + ts
    i = pl.program_id(0)

    @pl.when(i == 0)
    def _():
        ext_ref[0:POOL_HALO, :] = jnp.zeros((POOL_HALO, D_POOL), F32)

    @pl.when(i != 0)
    def _():
        ext_ref[0:POOL_HALO, :] = ext_ref[ts:rows, :]

    ext_ref[POOL_HALO:rows, :] = u_ref[...]
    t = i * ts + lax.broadcasted_iota(jnp.int32, (ts, 1), 0)
    for g, win in enumerate(POOL_WINDOWS):
        cols = slice(g * POOL_GROUP, (g + 1) * POOL_GROUP)
        u = ext_ref[POOL_HALO:rows, cols]
        src, src_cols, lo, shift, stage = ext_ref, cols, 0, 1, 0
        while True:
            lo += SUBLANES
            sums = src[lo:rows, src_cols] + src[lo - shift:rows - shift, src_cols]
            shift *= 2
            if shift == win:
                break
            dst = (sum_a_ref, sum_b_ref)[stage % 2]
            dst[lo:rows, :] = sums
            src, src_cols, stage = dst, slice(None), stage + 1
        win_sum = sums[POOL_HALO - lo:, :]
        cnt = jnp.minimum(t + 1, win).astype(F32)
        z = (win_sum / cnt - u).astype(BF16)
        y = jnp.dot(z, w_ref[g], preferred_element_type=F32)
        o_ref[:, cols] = (y * scale_ref[:, cols]).astype(o_ref.dtype)


def _pool(proj, w_pool, scale, ts=512):
    s = proj.shape[0]
    return pl.pallas_call(
        _pool_kernel,
        grid=(s // ts,),
        in_specs=[
            pl.BlockSpec((ts, D_POOL), lambda i: (i, 0)),
            pl.BlockSpec((len(POOL_WINDOWS), POOL_GROUP, POOL_GROUP), lambda i: (0, 0, 0)),
            pl.BlockSpec((1, D_POOL), lambda i: (0, 0)),
        ],
        out_specs=pl.BlockSpec((ts, D_POOL), lambda i: (i, 0)),
        out_shape=jax.ShapeDtypeStruct((s, D_POOL), BF16),
        scratch_shapes=[pltpu.VMEM((ts + POOL_HALO, D_POOL), F32),
                        pltpu.VMEM((ts + POOL_HALO, POOL_GROUP), F32), pltpu.VMEM((ts + POOL_HALO, POOL_GROUP), F32)],
        compiler_params=_params(("arbitrary",)),
        name="pool_mixer",
    )(proj, w_pool, scale)


def _mlstm_kernel(qk_ref, v_ref, o_ref, li_ref, b_ref, mx_ref, lirow_ref, brow_ref,
                  cw_ref, cb_ref, gh_ref, y_ref, ext_ref, c_ref, n_ref, m_ref):
    L = CHUNK
    i = pl.program_id(0)

    @pl.when(i == 0)
    def _():
        ext_ref[0:CONV_HALO, :] = jnp.zeros((CONV_HALO, D_QK), F32)
        c_ref[...] = jnp.zeros_like(c_ref)
        n_ref[...] = jnp.zeros_like(n_ref)
        m_ref[...] = jnp.zeros_like(m_ref)

    @pl.when(i != 0)
    def _():
        ext_ref[0:CONV_HALO, :] = ext_ref[L:L + CONV_HALO, :]

    ext_ref[CONV_HALO:CONV_HALO + L, :] = qk_ref[...]

    t_idx = lax.broadcasted_iota(jnp.int32, (L, L), 0)
    s_idx = lax.broadcasted_iota(jnp.int32, (L, L), 1)
    causal = s_idx <= t_idx

    def conv_silu(cols):
        y = cb_ref[:, cols]
        for j in range(CONV_W):
            off = CONV_HALO - (CONV_W - 1) + j
            y = y + cw_ref[j:j + 1, cols] * ext_ref[off:off + L, cols]
        return y * _sigmoid(y)

    for h in range(N_HEADS):
        q = conv_silu(slice(h * DQK, (h + 1) * DQK))
        k = conv_silu(slice(N_HEADS * DQK + h * DQK, N_HEADS * DQK + (h + 1) * DQK)) * (DQK ** -0.5)
        qb = q.astype(BF16)
        kb = k.astype(BF16)
        vcols = slice(h * DV, (h + 1) * DV)
        v = v_ref[:, vcols].astype(BF16)
        li_col = li_ref[:, h:h + 1]
        b_col = b_ref[:, h:h + 1]
        mx_col = mx_ref[:, h:h + 1]
        li_row = lirow_ref[0, h:h + 1, :]
        b_row = brow_ref[0, h:h + 1, :]
        m_prev = m_ref[h][:, 0:1]
        c_prev = c_ref[h]
        n_prev = n_ref[h]

        d_mat = jnp.where(causal, b_col - b_row + li_row, -jnp.inf)
        e_col = b_col + m_prev
        m_t = jnp.maximum(mx_col, e_col)
        s_qk = lax.dot_general(qb, kb, (((1,), (1,)), ((), ())), preferred_element_type=F32)
        w_mat = jnp.exp(d_mat - m_t) * s_qk
        w_inter = jnp.exp(e_col - m_t)
        num = jnp.dot(w_mat.astype(BF16), v, preferred_element_type=F32)
        num = num + w_inter * jnp.dot(qb, c_prev.astype(BF16), preferred_element_type=F32)
        den = jnp.sum(w_mat, axis=-1, keepdims=True)
        den = den + w_inter * jnp.sum(q * n_prev, axis=-1, keepdims=True)
        inv = 1.0 / jnp.maximum(jnp.abs(den), jnp.exp(-m_t))
        msq = jnp.mean(num * num, axis=-1, keepdims=True)
        scale = inv * lax.rsqrt(inv * inv * msq + EPS)
        hh = num * scale * (gh_ref[:, vcols] * _sigmoid(o_ref[:, vcols]))
        y_ref[:, vcols] = hh.astype(y_ref.dtype)

        g_tot = b_col[L - 1:L, :]
        a_col = g_tot + li_col - b_col
        m_loc = jnp.max(a_col, axis=0, keepdims=True)
        kw = k * jnp.exp(a_col - m_loc)
        c_loc = lax.dot_general(kw.astype(BF16), v, (((0,), (0,)), ((), ())), preferred_element_type=F32)
        n_loc = jnp.sum(kw, axis=0, keepdims=True)
        m_new = jnp.maximum(g_tot + m_prev, m_loc)
        s_old = jnp.exp(g_tot + m_prev - m_new)
        s_loc = jnp.exp(m_loc - m_new)
        c_ref[h] = s_old * c_prev + s_loc * c_loc
        n_ref[h] = s_old * n_prev + s_loc * n_loc
        m_ref[h] = jnp.broadcast_to(m_new, (1, LANES))


def _mlstm(proj, li, b, mx, lirow, brow, conv_w, conv_b, g_head):
    s = proj.shape[0]
    L = CHUNK
    col_spec = pl.BlockSpec((L, LANES), lambda i: (i, 0))
    row_spec = pl.BlockSpec((1, N_HEADS, L), lambda i: (i, 0, 0))
    return pl.pallas_call(
        _mlstm_kernel,
        grid=(s // L,),
        in_specs=[
            pl.BlockSpec((L, D_QK), lambda i: (i, D_POOL // D_QK)),
            pl.BlockSpec((L, D_MLSTM), lambda i: (i, (D_POOL + D_QK) // D_MLSTM)),
            pl.BlockSpec((L, D_MLSTM), lambda i: (i, (D_POOL + D_QK + D_MLSTM) // D_MLSTM)),
            col_spec, col_spec, col_spec, row_spec, row_spec,
            pl.BlockSpec((CONV_W, D_QK), lambda i: (0, 0)),
            pl.BlockSpec((1, D_QK), lambda i: (0, 0)),
            pl.BlockSpec((1, D_MLSTM), lambda i: (0, 0)),
        ],
        out_specs=pl.BlockSpec((L, D_MLSTM), lambda i: (i, 0)),
        out_shape=jax.ShapeDtypeStruct((s, D_MLSTM), BF16),
        scratch_shapes=[
            pltpu.VMEM((L + CONV_HALO, D_QK), F32),
            pltpu.VMEM((N_HEADS, DQK, DV), F32),
            pltpu.VMEM((N_HEADS, 1, DQK), F32),
            pltpu.VMEM((N_HEADS, 1, LANES), F32),
        ],
        compiler_params=_params(("arbitrary",)),
        name="mlstm_mixer",
    )(proj, proj, proj, li, b, mx, lirow, brow, conv_w, conv_b, g_head)


def kernel(x, p, g_mix, w_in, conv_w, conv_b, b_igate, b_fgate, g_head, w_pool, pool_scale,
           w_out, g_mlp, w_up, w_down, g_ple, w_ple_gate, b_ple_gate, w_ple, g_final):
    depth = w_in.shape[0]
    h = x.reshape(SEQ, D_MODEL)
    for l in range(depth):
        w_g = jnp.pad(w_in[l, :, D_MAIN:], ((0, 0), (0, LANES - 2 * N_HEADS))).astype(BF16)
        gate_bias = jnp.pad(jnp.concatenate([b_igate[l], b_fgate[l]]), (0, LANES - 2 * N_HEADS)).reshape(1, LANES)

        hn, li, b, mx, lirow, brow = _norm_gates(h, g_mix[l].reshape(1, D_MODEL), w_g, gate_bias.astype(F32))
        proj = _proj_in(l, hn, jnp.swapaxes(w_in, 1, 2))
        y_pool = _pool(proj, w_pool[l].astype(BF16), pool_scale[l].reshape(1, D_POOL))
        y_mlstm = _mlstm(proj, li, b, mx, lirow, brow, conv_w[l], conv_b[l].reshape(1, D_QK),
                         g_head[l].reshape(1, D_MLSTM))
        h, hb, r = _mix_out(l, y_pool, y_mlstm, w_out, h, g_mlp[l].reshape(1, D_MODEL))
        act, w_down_b = _mlp_up(l, hb, w_up, r, w_down)
        h, hb, ssq, w_gate_b = _down(l, act, w_down_b, h, w_ple_gate, g_ple[l].reshape(1, D_MODEL))

        assert depth == 1
        h = _ple_final(hb, w_gate_b, ssq, b_ple_gate[l].reshape(1, D_MODEL), p[l].reshape(SEQ, D_PLE),
                       w_ple[l].astype(BF16), h, g_final.reshape(1, D_MODEL))
    return h.reshape(x.shape)
```

```python
import jax
import jax.numpy as jnp
from jax import lax
from jax.experimental import pallas as pl
from jax.experimental.pallas import tpu as pltpu

F32 = jnp.float32
BF16 = jnp.bfloat16

D_MODEL = 4096
SEQ = 8192
D_POOL = 2048
POOL_WINDOWS = (2, 4, 8, 16)
POOL_GROUP = 512
SUBLANES = 8
POOL_HALO = 32
D_MLSTM = 2048
N_HEADS = 8
DV = 256
DQK = 128
D_QK = 2 * N_HEADS * DQK
CONV_W = 4
CONV_HALO = 8
GATE_CAP = 15.0
D_FF = 4 * D_MODEL
D_PLE = 256
EPS = 1e-6
D_MAIN = D_POOL + D_QK + 2 * D_MLSTM
LANES = 128
CHUNK = 256
VMEM_LIMIT = 58 * 1024 * 1024


def _params(sem, vmem=VMEM_LIMIT):
    return pltpu.CompilerParams(dimension_semantics=sem, vmem_limit_bytes=vmem)


def _chunk_scan(x, pos, combine, fill):
    sh = 1
    while sh < CHUNK:
        prev = jnp.where(pos >= sh, pltpu.roll(x, sh, 0), fill)
        x = combine(x, prev)
        sh *= 2
    return x


def _norm_gates_kernel(x_ref, g_ref, w_ref, bias_ref, hn_ref, cols_ref, rows_ref):
    tm = x_ref.shape[0]
    x = x_ref[...]
    ms = jnp.mean(x * x, axis=-1, keepdims=True)
    hn = (x * lax.rsqrt(ms + EPS) * g_ref[...]).astype(BF16)
    hn_ref[...] = hn
    pre = jnp.dot(hn, w_ref[...], preferred_element_type=F32) + bias_ref[...]
    pre = GATE_CAP * jnp.tanh(pre / GATE_CAP)
    logf = jnp.minimum(pre, 0.0) - jnp.log1p(jnp.exp(-jnp.abs(pre)))
    pos = lax.broadcasted_iota(jnp.int32, (tm, LANES), 0) % CHUNK
    b = _chunk_scan(logf, pos, jnp.add, 0.0)
    b = pltpu.roll(b, LANES - N_HEADS, 1)
    li = pre
    mx = b + _chunk_scan(li - b, pos, jnp.maximum, -jnp.inf)
    cols_ref[:, 0:LANES] = li
    cols_ref[:, LANES:2 * LANES] = b
    cols_ref[:, 2 * LANES:3 * LANES] = mx
    for c in range(tm // CHUNK):
        rows = slice(c * CHUNK, (c + 1) * CHUNK)
        rows_ref[c, 0:N_HEADS, :] = li[rows].T[:N_HEADS]
        rows_ref[c, N_HEADS:2 * N_HEADS, :] = b[rows].T[:N_HEADS]


def _norm_gates(x, g, w_g, bias, tm=512):
    m, d = x.shape
    col = jax.ShapeDtypeStruct((m, 3 * LANES), F32)
    row = jax.ShapeDtypeStruct((m // CHUNK, 2 * N_HEADS, CHUNK), F32)
    col_spec = pl.BlockSpec((tm, 3 * LANES), lambda i: (i, 0))
    row_spec = pl.BlockSpec((tm // CHUNK, 2 * N_HEADS, CHUNK), lambda i: (i, 0, 0))
    return pl.pallas_call(
        _norm_gates_kernel,
        grid=(m // tm,),
        in_specs=[
            pl.BlockSpec((tm, d), lambda i: (i, 0)),
            pl.BlockSpec((1, d), lambda i: (0, 0)),
            pl.BlockSpec((d, LANES), lambda i: (0, 0)),
            pl.BlockSpec((1, LANES), lambda i: (0, 0)),
        ],
        out_specs=[pl.BlockSpec((tm, d), lambda i: (i, 0)), col_spec, row_spec],
        out_shape=[jax.ShapeDtypeStruct((m, d), BF16), col, row],
        compiler_params=_params(("parallel",)),
        name="norm_gates",
    )(x, g, w_g, bias)


def _ws_maps(nb, n_i):
    row = lambda b, i: jnp.where(b > 0, i, 0)
    blk = lambda b: jnp.maximum(b - 1, 0)
    wrow = lambda b, i: jnp.where(b < nb, i, n_i - 1)
    wblk = lambda b: jnp.minimum(b, nb - 1)
    return row, blk, wrow, wblk


def _ws_step(wchunk_ref, wbufs, compute):
    b = pl.program_id(0)
    i = pl.program_id(1)

    def stage(buf):
        ck = wchunk_ref.shape[0]
        buf[pl.ds(pl.multiple_of(i * ck, ck), ck), :] = wchunk_ref[...].astype(BF16)

    @pl.when(b == 0)
    def _():
        stage(wbufs[0])

    for parity in (0, 1):
        @pl.when(jnp.logical_and(b > 0, b % 2 == parity))
        def _():
            stage(wbufs[parity])
            compute(wbufs[1 - parity])


def _row_sumsq(h):
    return jnp.broadcast_to(jnp.sum(h * h, axis=1, keepdims=True), (h.shape[0], LANES))


def _lane_tile(r, n):
    return jnp.concatenate([r] * (n // LANES), axis=1)


def _sigmoid(x):
    return 0.5 * jnp.tanh(0.5 * x) + 0.5


def _proj_kernel(a_ref, wchunk_ref, o_ref, wbuf0, wbuf1):
    b = pl.program_id(0)
    i = pl.program_id(1)
    wbufs = (wbuf0, wbuf1)

    def stage(buf):
        w = wchunk_ref[...].T
        ck = w.shape[0]
        buf[pl.ds(pl.multiple_of(i * ck, ck), ck), :] = w.astype(BF16)

    @pl.when(b == 0)
    def _():
        stage(wbuf0)

    for parity in (0, 1):
        @pl.when(jnp.logical_and(b > 0, b % 2 == parity))
        def _():
            stage(wbufs[parity])
            o_ref[...] = jnp.dot(a_ref[...], wbufs[1 - parity][...], preferred_element_type=F32)

def _mix_out_kernel(y_ref, wchunk_ref, res_ref, g_ref, o_ref, ob_ref, r_ref, wbuf0, wbuf1, ssq_ref):
    b = pl.program_id(0)
    tm = o_ref.shape[0]
    rows = pl.ds(pl.multiple_of(pl.program_id(1) * tm, tm), tm)

    @pl.when(b == 0)
    def _():
        ssq_ref[rows, :] = jnp.zeros((tm, LANES), F32)

    def compute(w_ref):
        h = res_ref[...] + jnp.dot(y_ref[...], w_ref[...], preferred_element_type=F32)
        o_ref[...] = h
        ob_ref[...] = (h * g_ref[...]).astype(BF16)
        total = ssq_ref[rows, :] + _row_sumsq(h)
        ssq_ref[rows, :] = total

        @pl.when(b == pl.num_programs(0) - 1)
        def _():
            r_ref[...] = lax.rsqrt(total * (1.0 / D_MODEL) + EPS)
    _ws_step(wchunk_ref, (wbuf0, wbuf1), compute)


def _up_kernel(a_ref, wchunk_ref, r_ref, wd_ref, o_ref, wdb_ref, wbuf0, wbuf1):
    def compute(w_ref):
        wdb_ref[...] = wd_ref[...].astype(BF16)
        y = jnp.dot(a_ref[...], w_ref[...], preferred_element_type=F32)
        y = jnp.maximum(y * _lane_tile(r_ref[...], y.shape[1]), 0.0)
        o_ref[...] = (y * y).astype(BF16)
    _ws_step(wchunk_ref, (wbuf0, wbuf1), compute)


def _ws_call(kern, name, l, a_list, w, k_dim, n_out, tm, tn, extra_in, extra_specs, out_shapes, out_specs,
             extra_scratch=()):
    m = a_list[0].shape[0]
    nb, n_i = n_out // tn, m // tm
    ck = k_dim // n_i
    row, blk, wrow, wblk = _ws_maps(nb, n_i)
    a_specs = [pl.BlockSpec((tm, a.shape[1]), lambda b, i: (row(b, i), 0)) for a in a_list]
    w_spec = pl.BlockSpec((None, ck, tn), lambda b, i: (l, wrow(b, i), wblk(b)))
    return pl.pallas_call(
        kern,
        grid=(nb + 1, n_i),
        in_specs=a_specs + [w_spec] + extra_specs(row, blk, wrow, wblk),
        out_specs=out_specs(row, blk),
        out_shape=out_shapes,
        scratch_shapes=[pltpu.VMEM((k_dim, tn), BF16), pltpu.VMEM((k_dim, tn), BF16), *extra_scratch],
        compiler_params=_params(("arbitrary", "arbitrary")),
        name=name,
    )(*a_list, w, *extra_in)


def _proj_in(l, hn, w_in_t, tm=1024, tn=1024):
    m = hn.shape[0]
    nb, n_i = D_MAIN // tn, m // tm
    ck = D_MODEL // n_i
    row, blk, wrow, wblk = _ws_maps(nb, n_i)
    return pl.pallas_call(
        _proj_kernel,
        grid=(nb + 1, n_i),
        in_specs=[
            pl.BlockSpec((tm, D_MODEL), lambda b, i: (row(b, i), 0)),
            pl.BlockSpec((None, tn, ck), lambda b, i: (l, wblk(b), wrow(b, i))),
        ],
        out_specs=pl.BlockSpec((tm, tn), lambda b, i: (row(b, i), blk(b))),
        out_shape=jax.ShapeDtypeStruct((m, D_MAIN), F32),
        scratch_shapes=[pltpu.VMEM((D_MODEL, tn), BF16), pltpu.VMEM((D_MODEL, tn), BF16)],
        compiler_params=_params(("arbitrary", "arbitrary")),
        name="proj_in",
    )(hn, w_in_t)


def _mix_out(l, y, w_out, resid, g_next, tm=512, tn=1024):
    m = resid.shape[0]
    nb = D_MODEL // tn
    tile = lambda row, blk: pl.BlockSpec((tm, tn), lambda b, i: (row(b, i), blk(b)))
    return _ws_call(
        _mix_out_kernel, "mix_out", l, [y], w_out, D_MODEL, D_MODEL, tm, tn, [resid, g_next],
        lambda row, blk, wrow, wblk: [tile(row, blk), pl.BlockSpec((1, tn), lambda b, i: (0, blk(b)))],
        [jax.ShapeDtypeStruct((m, D_MODEL), F32), jax.ShapeDtypeStruct((m, D_MODEL), BF16),
         jax.ShapeDtypeStruct((m, LANES), F32)],
        lambda row, blk: [tile(row, blk), tile(row, blk),
                          pl.BlockSpec((tm, LANES), lambda b, i: (jnp.where(b == nb, i, 0), 0))],
        extra_scratch=[pltpu.VMEM((m, LANES), F32)])


def _mlp_up(l, hb, w_up, r, w_down, tm=1024, tn=1024):
    m = hb.shape[0]
    nb, n_i = D_FF // tn, m // tm
    wd_rows = D_FF // (nb * n_i)
    side = lambda row, blk: (lambda b, i: (blk(b) * n_i + row(b, i), 0))
    return _ws_call(
        _up_kernel, "mlp_up", l, [hb], w_up, D_MODEL, D_FF, tm, tn, [r, w_down],
        lambda row, blk, wrow, wblk: [
            pl.BlockSpec((tm, LANES), lambda b, i: (row(b, i), 0)),
            pl.BlockSpec((None, wd_rows, D_MODEL), lambda b, i: (l,) + side(row, blk)(b, i)),
        ],
        [jax.ShapeDtypeStruct((m, D_FF), BF16), jax.ShapeDtypeStruct((D_FF, D_MODEL), BF16)],
        lambda row, blk: [pl.BlockSpec((tm, tn), lambda b, i: (row(b, i), blk(b))),
                          pl.BlockSpec((wd_rows, D_MODEL), side(row, blk))])


def _ple_kernel(a_ref, w_ref, ssq_ref, bias_ref, p_ref, wp_ref, res_hbm, gf_ref, o_ref, res_sem):
    tm, d = o_ref.shape
    rows = pl.ds(pl.multiple_of(pl.program_id(0) * tm, tm), tm)
    res_copy = pltpu.make_async_copy(res_hbm.at[rows, :], o_ref, res_sem)
    res_copy.start()
    s = ssq_ref[:, 0:LANES]
    for c in range(1, ssq_ref.shape[1] // LANES):
        s = s + ssq_ref[:, c * LANES:(c + 1) * LANES]
    r = lax.rsqrt(s * (1.0 / D_MODEL) + EPS)
    z = jnp.dot(a_ref[...], w_ref[...], preferred_element_type=F32)
    z = z * _lane_tile(r, d) + bias_ref[...]
    update = jax.nn.sigmoid(z) * jnp.dot(p_ref[...].astype(BF16), wp_ref[...], preferred_element_type=F32)
    res_copy.wait()
    x = o_ref[...] + update
    ms = jnp.mean(x * x, axis=-1, keepdims=True)
    o_ref[...] = x * lax.rsqrt(ms + EPS) * gf_ref[...]


def _ple_final(hb, w_gate_b, ssq, b_gate, p, w_ple, resid, g_final, tm=256):
    m = hb.shape[0]
    resident = pl.Buffered(1)
    return pl.pallas_call(
        _ple_kernel,
        grid=(m // tm,),
        in_specs=[
            pl.BlockSpec((tm, D_MODEL), lambda i: (i, 0)),
            pl.BlockSpec((D_MODEL, D_MODEL), lambda i: (0, 0), pipeline_mode=resident),
            pl.BlockSpec((tm, ssq.shape[1]), lambda i: (i, 0)),
            pl.BlockSpec((1, D_MODEL), lambda i: (0, 0)),
            pl.BlockSpec((tm, D_PLE), lambda i: (i, 0)),
            pl.BlockSpec((D_PLE, D_MODEL), lambda i: (0, 0), pipeline_mode=resident),
            pl.BlockSpec(memory_space=pl.ANY),
            pl.BlockSpec((1, D_MODEL), lambda i: (0, 0)),
        ],
        out_specs=pl.BlockSpec((tm, D_MODEL), lambda i: (i, 0)),
        out_shape=jax.ShapeDtypeStruct((m, D_MODEL), F32),
        scratch_shapes=[pltpu.SemaphoreType.DMA(())],
        compiler_params=_params(("arbitrary",)),
        name="ple_final",
    )(hb, w_gate_b, ssq, b_gate, p, w_ple, resid, g_final)


def _down_kernel(a_ref, w_ref, res_hbm, wg_ref, g_ref, o_ref, ob_ref, ssq_ref, wgb_ref, res_sem):
    i, j, k = pl.program_id(0), pl.program_id(1), pl.program_id(2)
    last = pl.num_programs(2) - 1
    tm, tn = o_ref.shape
    wgb_ref[...] = wg_ref[...].astype(BF16)

    def part():
        return jnp.dot(a_ref[...], w_ref[...], preferred_element_type=F32)

    @pl.when(k == 0)
    def _():
        rows = pl.ds(pl.multiple_of(i * tm, tm), tm)
        cols = pl.ds(pl.multiple_of(j * tn, tn), tn)
        res_copy = pltpu.make_async_copy(res_hbm.at[rows, cols], o_ref, res_sem)
        res_copy.start()
        p = part()
        res_copy.wait()
        o_ref[...] += p

    @pl.when(jnp.logical_and(k != 0, k != last))
    def _():
        o_ref[...] += part()

    @pl.when(k == last)
    def _():
        h = o_ref[...] + part()
        o_ref[...] = h
        ob_ref[...] = (h * g_ref[...]).astype(BF16)
        ssq_ref[...] = _row_sumsq(h)


def _down(l, a, w, resid, w_gate, g_next, tm=1024, tn=1024, tk=4096):
    m, k = a.shape
    n = w.shape[1]
    n_j, n_k = n // tn, k // tk
    side_rows = w_gate.shape[1] // ((m // tm) * n_j * n_k)
    tile = pl.BlockSpec((tm, tn), lambda i, j, kk: (i, j))
    step = lambda i, j, kk: (i * n_j + j) * n_k + kk
    return pl.pallas_call(
        _down_kernel,
        grid=(m // tm, n_j, n_k),
        in_specs=[
            pl.BlockSpec((tm, tk), lambda i, j, kk: (i, kk)),
            pl.BlockSpec((tk, tn), lambda i, j, kk: (kk, j)),
            pl.BlockSpec(memory_space=pl.ANY),
            pl.BlockSpec((None, side_rows, w_gate.shape[2]), lambda i, j, kk: (l, step(i, j, kk), 0)),
            pl.BlockSpec((1, tn), lambda i, j, kk: (0, j)),
        ],
        out_specs=[tile, tile, pl.BlockSpec((tm, LANES), lambda i, j, kk: (i, j)),
                   pl.BlockSpec((side_rows, w_gate.shape[2]), lambda i, j, kk: (step(i, j, kk), 0))],
        out_shape=[jax.ShapeDtypeStruct((m, n), F32), jax.ShapeDtypeStruct((m, n), BF16),
                   jax.ShapeDtypeStruct((m, n_j * LANES), F32),
                   jax.ShapeDtypeStruct(w_gate.shape[1:], BF16)],
        scratch_shapes=[pltpu.SemaphoreType.DMA(())],
        compiler_params=_params(("arbitrary", "arbitrary", "arbitrary")),
        name="mlp_down",
    )(a, w, resid, w_gate, g_next)


def _pool_kernel(u_ref, w_ref, scale_ref, o_ref, ext_ref, sum_a_ref, sum_b_ref):
    ts = u_ref.shape[0]
    rows = POOL_HALO + ts
    i = pl.program_id(0)

    @pl.when(i == 0)
    def _():
        ext_ref[0:POOL_HALO, :] = jnp.zeros((POOL_HALO, D_POOL), F32)

    @pl.when(i != 0)
    def _():
        ext_ref[0:POOL_HALO, :] = ext_ref[ts:rows, :]

    ext_ref[POOL_HALO:rows, :] = u_ref[...]
    t = i * ts + lax.broadcasted_iota(jnp.int32, (ts, 1), 0)
    yield
    for g, win in enumerate(POOL_WINDOWS):
        cols = slice(g * POOL_GROUP, (g + 1) * POOL_GROUP)
        u = ext_ref[POOL_HALO:rows, cols]
        src, src_cols, lo, shift, stage = ext_ref, cols, 0, 1, 0
        while True:
            lo += SUBLANES
            sums = src[lo:rows, src_cols] + src[lo - shift:rows - shift, src_cols]
            shift *= 2
            if shift == win:
                break
            dst = (sum_a_ref, sum_b_ref)[stage % 2]
            dst[lo:rows, :] = sums
            src, src_cols, stage = dst, slice(None), stage + 1
        win_sum = sums[POOL_HALO - lo:, :]
        cnt = jnp.minimum(t + 1, win).astype(F32)
        z = (win_sum / cnt - u).astype(BF16)
        y = jnp.dot(z, w_ref[g], preferred_element_type=F32)
        o_ref[:, cols] = (y * scale_ref[:, cols]).astype(o_ref.dtype)
        yield


def _mlstm_kernel(qk_ref, v_ref, o_ref, li_ref, b_ref, mx_ref, lirow_ref, brow_ref,
                  cw_ref, cb_ref, gh_ref, y_ref, ext_ref, c_ref, n_ref, m_ref):
    L = CHUNK
    i = pl.program_id(0)

    @pl.when(i == 0)
    def _():
        ext_ref[0:CONV_HALO, :] = jnp.zeros((CONV_HALO, D_QK), F32)
        c_ref[...] = jnp.zeros_like(c_ref)
        n_ref[...] = jnp.zeros_like(n_ref)
        m_ref[...] = jnp.zeros_like(m_ref)

    @pl.when(i != 0)
    def _():
        ext_ref[0:CONV_HALO, :] = ext_ref[L:L + CONV_HALO, :]

    ext_ref[CONV_HALO:CONV_HALO + L, :] = qk_ref[...]

    t_idx = lax.broadcasted_iota(jnp.int32, (L, L), 0)
    s_idx = lax.broadcasted_iota(jnp.int32, (L, L), 1)
    causal = s_idx <= t_idx

    def conv_silu(cols):
        y = cb_ref[:, cols]
        for j in range(CONV_W):
            off = CONV_HALO - (CONV_W - 1) + j
            y = y + cw_ref[j:j + 1, cols] * ext_ref[off:off + L, cols]
        return y * _sigmoid(y)

    yield
    for h in range(N_HEADS):
        q = conv_silu(slice(h * DQK, (h + 1) * DQK))
        k = conv_silu(slice(N_HEADS * DQK + h * DQK, N_HEADS * DQK + (h + 1) * DQK)) * (DQK ** -0.5)
        qb = q.astype(BF16)
        kb = k.astype(BF16)
        vcols = slice(h * DV, (h + 1) * DV)
        v = v_ref[:, vcols].astype(BF16)
        li_col = li_ref[:, h:h + 1]
        b_col = b_ref[:, h:h + 1]
        mx_col = mx_ref[:, h:h + 1]
        li_row = lirow_ref[0, h:h + 1, :]
        b_row = brow_ref[0, h:h + 1, :]
        m_prev = m_ref[h][:, 0:1]
        c_prev = c_ref[h]
        n_prev = n_ref[h]

        d_mat = jnp.where(causal, b_col - b_row + li_row, -jnp.inf)
        e_col = b_col + m_prev
        m_t = jnp.maximum(mx_col, e_col)
        s_qk = lax.dot_general(qb, kb, (((1,), (1,)), ((), ())), preferred_element_type=F32)
        w_mat = jnp.exp(d_mat - m_t) * s_qk
        w_inter = jnp.exp(e_col - m_t)
        num = jnp.dot(w_mat.astype(BF16), v, preferred_element_type=F32)
        num = num + w_inter * jnp.dot(qb, c_prev.astype(BF16), preferred_element_type=F32)
        den = jnp.sum(w_mat, axis=-1, keepdims=True)
        den = den + w_inter * jnp.sum(q * n_prev, axis=-1, keepdims=True)
        inv = 1.0 / jnp.maximum(jnp.abs(den), jnp.exp(-m_t))
        msq = jnp.mean(num * num, axis=-1, keepdims=True)
        scale = inv * lax.rsqrt(inv * inv * msq + EPS)
        hh = num * scale * (gh_ref[:, vcols] * _sigmoid(o_ref[:, vcols]))
        y_ref[:, vcols] = hh.astype(y_ref.dtype)

        g_tot = b_col[L - 1:L, :]
        a_col = g_tot + li_col - b_col
        m_loc = jnp.max(a_col, axis=0, keepdims=True)
        kw = k * jnp.exp(a_col - m_loc)
        c_loc = lax.dot_general(kw.astype(BF16), v, (((0,), (0,)), ((), ())), preferred_element_type=F32)
        n_loc = jnp.sum(kw, axis=0, keepdims=True)
        m_new = jnp.maximum(g_tot + m_prev, m_loc)
        s_old = jnp.exp(g_tot + m_prev - m_new)
        s_loc = jnp.exp(m_loc - m_new)
        c_ref[h] = s_old * c_prev + s_loc * c_loc
        n_ref[h] = s_old * n_prev + s_loc * n_loc
        m_ref[h] = jnp.broadcast_to(m_new, (1, LANES))
        yield


def _mixers_kernel(p_ref, wp_ref, ps_ref, cols_ref, rows_ref, cw_ref, cb_ref, gh_ref, y_ref,
                   pool_ext_ref, sum_a_ref, sum_b_ref, conv_ext_ref, c_ref, n_ref, m_ref):
    o0, o1, o2 = D_POOL, D_POOL + D_QK, D_POOL + D_QK + D_MLSTM
    pool = _pool_kernel(p_ref.at[:, 0:o0], wp_ref, ps_ref, y_ref.at[:, 0:D_POOL], pool_ext_ref, sum_a_ref, sum_b_ref)
    mlstm = _mlstm_kernel(p_ref.at[:, o0:o1], p_ref.at[:, o1:o2], p_ref.at[:, o2:o2 + D_MLSTM],
                          cols_ref.at[:, 0:LANES], cols_ref.at[:, LANES:2 * LANES], cols_ref.at[:, 2 * LANES:3 * LANES],
                          rows_ref.at[:, 0:N_HEADS, :], rows_ref.at[:, N_HEADS:2 * N_HEADS, :], cw_ref, cb_ref, gh_ref,
                          y_ref.at[:, D_POOL:D_POOL + D_MLSTM], conv_ext_ref, c_ref, n_ref, m_ref)
    next(pool)
    next(mlstm)
    heads_per_group = N_HEADS // len(POOL_WINDOWS)
    for h in range(N_HEADS):
        next(mlstm)
        if h % heads_per_group == heads_per_group - 1:
            next(pool)


def _mixers(proj, gate_cols, gate_rows, conv_w, conv_b, g_head, w_pool, pool_scale):
    s = proj.shape[0]
    L = CHUNK
    return pl.pallas_call(
        _mixers_kernel,
        grid=(s // L,),
        in_specs=[
            pl.BlockSpec((L, D_MAIN), lambda i: (i, 0)),
            pl.BlockSpec((len(POOL_WINDOWS), POOL_GROUP, POOL_GROUP), lambda i: (0, 0, 0)),
            pl.BlockSpec((1, D_POOL), lambda i: (0, 0)),
            pl.BlockSpec((L, gate_cols.shape[1]), lambda i: (i, 0)),
            pl.BlockSpec((1,) + gate_rows.shape[1:], lambda i: (i, 0, 0)),
            pl.BlockSpec((CONV_W, D_QK), lambda i: (0, 0)),
            pl.BlockSpec((1, D_QK), lambda i: (0, 0)),
            pl.BlockSpec((1, D_MLSTM), lambda i: (0, 0)),
        ],
        out_specs=pl.BlockSpec((L, D_POOL + D_MLSTM), lambda i: (i, 0)),
        out_shape=jax.ShapeDtypeStruct((s, D_POOL + D_MLSTM), BF16),
        scratch_shapes=[
            pltpu.VMEM((L + POOL_HALO, D_POOL), F32),
            pltpu.VMEM((L + POOL_HALO, POOL_GROUP), F32),
            pltpu.VMEM((L + POOL_HALO, POOL_GROUP), F32),
            pltpu.VMEM((L + CONV_HALO, D_QK), F32),
            pltpu.VMEM((N_HEADS, DQK, DV), F32),
            pltpu.VMEM((N_HEADS, 1, DQK), F32),
            pltpu.VMEM((N_HEADS, 1, LANES), F32),
        ],
        compiler_params=_params(("arbitrary",)),
        name="mixers",
    )(proj, w_pool, pool_scale, gate_cols, gate_rows, conv_w, conv_b, g_head)


def kernel(x, p, g_mix, w_in, conv_w, conv_b, b_igate, b_fgate, g_head, w_pool, pool_scale,
           w_out, g_mlp, w_up, w_down, g_ple, w_ple_gate, b_ple_gate, w_ple, g_final):
    depth = w_in.shape[0]
    h = x.reshape(SEQ, D_MODEL)
    for l in range(depth):
        w_g = jnp.pad(w_in[l, :, D_MAIN:], ((0, 0), (0, LANES - 2 * N_HEADS))).astype(BF16)
        gate_bias = jnp.pad(jnp.concatenate([b_igate[l], b_fgate[l]]), (0, LANES - 2 * N_HEADS)).reshape(1, LANES)

        hn, gate_cols, gate_rows = _norm_gates(h, g_mix[l].reshape(1, D_MODEL), w_g, gate_bias.astype(F32))
        proj = _proj_in(l, hn, jnp.swapaxes(w_in, 1, 2))
        y = _mixers(proj, gate_cols, gate_rows, conv_w[l], conv_b[l].reshape(1, D_QK), g_head[l].reshape(1, D_MLSTM),
                    w_pool[l].astype(BF16), pool_scale[l].reshape(1, D_POOL))
        h, hb, r = _mix_out(l, y, w_out, h, g_mlp[l].reshape(1, D_MODEL))
        act, w_down_b = _mlp_up(l, hb, w_up, r, w_down)
        h, hb, ssq, w_gate_b = _down(l, act, w_down_b, h, w_ple_gate, g_ple[l].reshape(1, D_MODEL))

        assert depth == 1
        h = _ple_final(hb, w_gate_b, ssq, b_ple_gate[l].reshape(1, D_MODEL), p[l].reshape(SEQ, D_PLE),
                       w_ple[l].astype(BF16), h, g_final.reshape(1, D_MODEL))
    return h.reshape(x.shape)
```
